```python
import jax, jax.numpy as jnp
from jax import lax
import numpy as np

D_MODEL = 1024
BATCH = 4
SEQ = 4096
DEPTH = 4

N_MIXERS = 3
SHORT_CONV_W = 3
POOL_WINDOWS = (2, 4, 8, 16)
N_POOL_GROUPS = len(POOL_WINDOWS)
POOL_GROUP_W = D_MODEL // N_POOL_GROUPS
CONF_CONV_W = 31
D_FF = ((8 * D_MODEL // 3 + 255) // 256) * 256
RMS_EPS = 1e-6
LN_EPS = 1e-5

kernel_name = "hybrid_interleaved_conv_pool_conformer"


def rmsnorm(x, g):
    xf = x.astype(jnp.float32)
    y = xf * lax.rsqrt(jnp.mean(xf * xf, axis=-1, keepdims=True) + RMS_EPS)
    return (y * g.astype(jnp.float32)).astype(x.dtype)


def layernorm(x, g, b):
    xf = x.astype(jnp.float32)
    mu = jnp.mean(xf, axis=-1, keepdims=True)
    xc = xf - mu
    var = jnp.mean(xc * xc, axis=-1, keepdims=True)
    y = xc * lax.rsqrt(var + LN_EPS) * g.astype(jnp.float32) + b.astype(jnp.float32)
    return y.astype(x.dtype)


def causal_depthwise_conv(u, w):
    k, c = w.shape
    return lax.conv_general_dilated(
        u, w[:, None, :].astype(u.dtype),
        window_strides=(1,), padding=((k - 1, 0),),
        dimension_numbers=("NWC", "WIO", "NWC"),
        feature_group_count=c)


def short_conv_mixer(u, w_in, conv_w, w_out):
    b, c, v = jnp.split(u @ w_in, 3, axis=-1)
    return (b * causal_depthwise_conv(c * v, conv_w)) @ w_out


def multiscale_pool_mixer(u, w_grp, scale):
    bn, s, d = u.shape
    ug = u.reshape(bn, s, N_POOL_GROUPS, POOL_GROUP_W).astype(jnp.float32)
    cs = jnp.cumsum(ug, axis=1)
    t = jnp.arange(s)
    pooled = []
    for g, w in enumerate(POOL_WINDOWS):
        c = cs[:, :, g, :]
        lag = jnp.pad(c, ((0, 0), (w, 0), (0, 0)))[:, :s]
        cnt = jnp.minimum(t + 1, w).astype(jnp.float32)[None, :, None]
        pooled.append((c - lag) / cnt)
    mixed = (jnp.stack(pooled, axis=2) - ug).astype(u.dtype)
    y = jnp.einsum("bsgc,gcd->bsgd", mixed, w_grp)
    return y.reshape(bn, s, d) * scale


def conformer_conv_module(u, w_pw1, b_pw1, w_dw, b_dw, ln_g, ln_b, w_pw2, b_pw2):
    a, gate = jnp.split(u @ w_pw1 + b_pw1, 2, axis=-1)
    h = a * jax.nn.sigmoid(gate)
    h = causal_depthwise_conv(h, w_dw) + b_dw
    h = jax.nn.silu(layernorm(h, ln_g, ln_b))
    return h @ w_pw2 + b_pw2


def swiglu_ffn(u, w_gu, w_down):
    g, up = jnp.split(u @ w_gu, 2, axis=-1)
    return (jax.nn.silu(g) * up) @ w_down


def setup_inputs(seed: int = 0) -> dict:
    key = jax.random.key(seed)
    keys = iter(jax.random.split(key, 64))
    d, f = D_MODEL, D_FF

    def nrm(shape, scale):
        return jax.random.normal(next(keys), shape, jnp.float32) * scale

    def gain(n):
        return 1.0 + nrm((n,), 0.05)

    def short_conv_params(i):
        return {f"a{i}_w_in": nrm((d, 3 * d), d ** -0.5),
                f"a{i}_conv": nrm((SHORT_CONV_W, d), SHORT_CONV_W ** -0.5),
                f"a{i}_w_out": nrm((d, d), d ** -0.5)}

    def ffn_params(i):
        return {f"ln2_{i}": gain(d),
                f"ffn{i}_w_gu": nrm((d, 2 * f), d ** -0.5),
                f"ffn{i}_w_down": nrm((f, d), f ** -0.5)}

    p = {"x": nrm((BATCH, SEQ, d), 1.0)}
    p["ln1_0"] = gain(d)
    p.update(short_conv_params(0))
    p.update(ffn_params(0))
    p["ln1_1"] = gain(d)
    p["b1_w_grp"] = nrm((N_POOL_GROUPS, POOL_GROUP_W, POOL_GROUP_W), POOL_GROUP_W ** -0.5)
    p["b1_scale"] = 1.0 + nrm((d,), 0.1)
    p.update(ffn_params(1))
    p["ln1_2"] = gain(d)
    p["c2_w_pw1"] = nrm((d, 2 * d), d ** -0.5)
    p["c2_b_pw1"] = nrm((2 * d,), 0.02)
    p["c2_dw"] = nrm((CONF_CONV_W, d), CONF_CONV_W ** -0.5)
    p["c2_b_dw"] = nrm((d,), 0.02)
    p["c2_ln_g"] = gain(d)
    p["c2_ln_b"] = nrm((d,), 0.02)
    p["c2_w_pw2"] = nrm((d, d), d ** -0.5)
    p["c2_b_pw2"] = nrm((d,), 0.02)
    p.update(ffn_params(2))
    p["ln1_3"] = gain(d)
    p.update(short_conv_params(3))
    p.update(ffn_params(3))
    p["ln_f"] = gain(d)
    return p


def reference(x,
              ln1_0, a0_w_in, a0_conv, a0_w_out, ln2_0, ffn0_w_gu, ffn0_w_down,
              ln1_1, b1_w_grp, b1_scale, ln2_1, ffn1_w_gu, ffn1_w_down,
              ln1_2, c2_w_pw1, c2_b_pw1, c2_dw, c2_b_dw, c2_ln_g, c2_ln_b, c2_w_pw2, c2_b_pw2,
              ln2_2, ffn2_w_gu, ffn2_w_down,
              ln1_3, a3_w_in, a3_conv, a3_w_out, ln2_3, ffn3_w_gu, ffn3_w_down,
              ln_f):
    mixer_fns = (short_conv_mixer, multiscale_pool_mixer, conformer_conv_module)
    mixer_params = (
        (a0_w_in, a0_conv, a0_w_out),
        (b1_w_grp, b1_scale),
        (c2_w_pw1, c2_b_pw1, c2_dw, c2_b_dw, c2_ln_g, c2_ln_b, c2_w_pw2, c2_b_pw2),
        (a3_w_in, a3_conv, a3_w_out),
    )
    pre_mix_norms = (ln1_0, ln1_1, ln1_2, ln1_3)
    pre_ffn_norms = (ln2_0, ln2_1, ln2_2, ln2_3)
    ffn_params = ((ffn0_w_gu, ffn0_w_down), (ffn1_w_gu, ffn1_w_down),
                  (ffn2_w_gu, ffn2_w_down), (ffn3_w_gu, ffn3_w_down))

    h = x
    for i in range(DEPTH):
        mixer = mixer_fns[i % N_MIXERS]
        h = h + mixer(rmsnorm(h, pre_mix_norms[i]), *mixer_params[i])
        h = h + swiglu_ffn(rmsnorm(h, pre_ffn_norms[i]), *ffn_params[i])
    return rmsnorm(h, ln_f)
```

```python
import functools

import jax
import jax.numpy as jnp
from jax import lax
from jax.experimental import pallas as pl
from jax.experimental.pallas import tpu as pltpu

RMS_EPS = 1e-6
LN_EPS = 1e-5
POOL_WINDOWS = (2, 4, 8, 16)

SEQ_TILE = 512
FFN_CHUNK = 256
SUBLANES = 8
VMEM_LIMIT = 52 * 1024 * 1024

_BF16 = jnp.bfloat16
_F32 = jnp.float32


def _dot(a, b):
    return jnp.dot(a, b, preferred_element_type=_F32)


def _rmsnorm(x, g):
    ms = jnp.mean(x * x, axis=-1, keepdims=True)
    return x * lax.rsqrt(ms + RMS_EPS) * g


def _resident(shape):
    zeros = (0,) * len(shape)
    return pl.BlockSpec(shape, lambda *_: zeros, pipeline_mode=pl.Buffered(1))


def _row_tile(d):
    return pl.BlockSpec((None, SEQ_TILE, d), lambda b, s: (b, s, 0))


def _params(n_axes):
    return pltpu.CompilerParams(
        dimension_semantics=("arbitrary",) * n_axes,
        vmem_limit_bytes=VMEM_LIMIT)


def _ffn_kernel(h_ref, g_ref, wgu_ref, wdown_ref, lnf_ref, o_ref, act_ref, *, final_norm):
    f = wdown_ref.shape[0]
    x = h_ref[...]
    xn = _rmsnorm(x, g_ref[...]).astype(_BF16)
    for j in range(f // FFN_CHUNK):
        lo = j * FFN_CHUNK
        gate = _dot(xn, wgu_ref[:, lo:lo + FFN_CHUNK])
        up = _dot(xn, wgu_ref[:, f + lo:f + lo + FFN_CHUNK])
        act_ref[:, lo:lo + FFN_CHUNK] = (gate * jax.nn.sigmoid(gate) * up).astype(_BF16)
    y = x + _dot(act_ref[...], wdown_ref[...])
    if final_norm:
        y = _rmsnorm(y, lnf_ref[...])
    o_ref[...] = y


def _ffn(h, g, w_gu, w_down, ln_f, final_norm):
    bn, s, d = h.shape
    f = w_down.shape[0]
    return pl.pallas_call(
        functools.partial(_ffn_kernel, final_norm=final_norm),
        grid=(bn, s // SEQ_TILE),
        in_specs=[_row_tile(d), _resident((1, d)), _resident((d, 2 * f)),
                  _resident((f, d)), _resident((1, d))],
        out_specs=_row_tile(d),
        out_shape=jax.ShapeDtypeStruct(h.shape, h.dtype),
        scratch_shapes=[pltpu.VMEM((SEQ_TILE, f), _BF16)],
        compiler_params=_params(2),
        name="ffn_final" if final_norm else "ffn",
    )(h, g, w_gu, w_down, ln_f)


def _short_conv_kernel(h_ref, g_ref, win_ref, cw_ref, wout_ref, o_ref, u_ref):
    d = h_ref.shape[-1]
    hist = SUBLANES

    @pl.when(pl.program_id(1) == 0)
    def _():
        u_ref[0:hist, :] = jnp.zeros((hist, d), _F32)

    x = h_ref[...]
    xn = _rmsnorm(x, g_ref[...]).astype(_BF16)
    bcv = _dot(xn, win_ref[...])
    gate_b = bcv[:, 0:d]
    u_ref[hist:hist + SEQ_TILE, :] = bcv[:, d:2 * d] * bcv[:, 2 * d:3 * d]
    k = cw_ref.shape[0]
    conv = cw_ref[k - 1:k, :] * u_ref[hist:hist + SEQ_TILE, :]
    for j in range(1, k):
        conv = conv + cw_ref[k - 1 - j:k - j, :] * u_ref[hist - j:hist - j + SEQ_TILE, :]
    o_ref[...] = x + _dot((gate_b * conv).astype(_BF16), wout_ref[...])
    u_ref[0:hist, :] = u_ref[SEQ_TILE:SEQ_TILE + hist, :]


def _short_conv(h, g, w_in, conv_w, w_out):
    bn, s, d = h.shape
    return pl.pallas_call(
        _short_conv_kernel,
        grid=(bn, s // SEQ_TILE),
        in_specs=[_row_tile(d), _resident((1, d)), _resident((d, 3 * d)),
                  _resident(conv_w.shape), _resident((d, d))],
        out_specs=_row_tile(d),
        out_shape=jax.ShapeDtypeStruct(h.shape, h.dtype),
        scratch_shapes=[pltpu.VMEM((SUBLANES + SEQ_TILE, d), _F32)],
        compiler_params=_params(2),
        name="short_conv",
    )(h, g, w_in, conv_w, w_out)


def _pool_kernel(h_ref, g_ref, wgrp_ref, scale_ref, o_ref, xn_ref):
    d = h_ref.shape[-1]
    hist = max(POOL_WINDOWS)
    gw = d // len(POOL_WINDOWS)

    @pl.when(pl.program_id(1) == 0)
    def _():
        xn_ref[0:hist, :] = jnp.zeros((hist, d), _F32)

    x = h_ref[...]
    xn_ref[hist:hist + SEQ_TILE, :] = _rmsnorm(x, g_ref[...])
    t = pl.program_id(1) * SEQ_TILE + lax.broadcasted_iota(jnp.int32, (SEQ_TILE, 1), 0)
    for gi, w in enumerate(POOL_WINDOWS):
        c0 = gi * gw
        cur = xn_ref[hist:hist + SEQ_TILE, c0:c0 + gw]
        tot = cur
        for j in range(1, w):
            tot = tot + xn_ref[hist - j:hist - j + SEQ_TILE, c0:c0 + gw]
        cnt = jnp.minimum(t + 1, w).astype(_F32)
        mixed = (tot / cnt - cur).astype(_BF16)
        y = _dot(mixed, wgrp_ref[gi]) * scale_ref[:, c0:c0 + gw]
        o_ref[:, c0:c0 + gw] = x[:, c0:c0 + gw] + y
    xn_ref[0:hist, :] = xn_ref[SEQ_TILE:SEQ_TILE + hist, :]


def _pool(h, g, w_grp, scale):
    bn, s, d = h.shape
    hist = max(POOL_WINDOWS)
    return pl.pallas_call(
        _pool_kernel,
        grid=(bn, s // SEQ_TILE),
        in_specs=[_row_tile(d), _resident((1, d)), _resident(w_grp.shape), _resident((1, d))],
        out_specs=_row_tile(d),
        out_shape=jax.ShapeDtypeStruct(h.shape, h.dtype),
        scratch_shapes=[pltpu.VMEM((hist + SEQ_TILE, d), _F32)],
        compiler_params=_params(2),
        name="pool",
    )(h, g, w_grp, scale)


CONF_ROWS = 32


def _conformer_kernel(h_ref, g_ref, w1_ref, b1_ref, dw_ref, bdw_ref, lng_ref, lnb_ref,
                      w2_ref, b2_ref, o_ref, glu_ref, z_ref):
    d = h_ref.shape[-1]
    k = dw_ref.shape[0]
    hist = glu_ref.shape[0] - SEQ_TILE

    @pl.when(pl.program_id(1) == 0)
    def _():
        glu_ref[0:hist, :] = jnp.zeros((hist, d), _F32)

    x = h_ref[...]
    xn = _rmsnorm(x, g_ref[...]).astype(_BF16)
    ag = _dot(xn, w1_ref[...]) + b1_ref[...]
    glu_ref[hist:hist + SEQ_TILE, :] = ag[:, 0:d] * jax.nn.sigmoid(ag[:, d:2 * d])

    for r0 in range(0, SEQ_TILE, CONF_ROWS):
        acc = jnp.zeros((CONF_ROWS, d), _F32) + bdw_ref[...]
        for j in range(k):
            acc = acc + dw_ref[j:j + 1, :] * glu_ref[pl.ds(r0 + hist - (k - 1) + j, CONF_ROWS), :]
        mu = jnp.mean(acc, axis=-1, keepdims=True)
        xc = acc - mu
        var = jnp.mean(xc * xc, axis=-1, keepdims=True)
        ln = xc * lax.rsqrt(var + LN_EPS) * lng_ref[...] + lnb_ref[...]
        z_ref[pl.ds(r0, CONF_ROWS), :] = (ln * jax.nn.sigmoid(ln)).astype(_BF16)
    o_ref[...] = x + _dot(z_ref[...], w2_ref[...]) + b2_ref[...]
    glu_ref[0:hist, :] = glu_ref[SEQ_TILE:SEQ_TILE + hist, :]


def _conformer(h, g, w1, b1, dw, bdw, lng, lnb, w2, b2):
    bn, s, d = h.shape
    hist = -(-(dw.shape[0] - 1) // SUBLANES) * SUBLANES
    return pl.pallas_call(
        _conformer_kernel,
        grid=(bn, s // SEQ_TILE),
        in_specs=[_row_tile(d), _resident((1, d)), _resident((d, 2 * d)), _resident((1, 2 * d)),
                  _resident(dw.shape), _resident((1, d)), _resident((1, d)), _resident((1, d)),
                  _resident((d, d)), _resident((1, d))],
        out_specs=_row_tile(d),
        out_shape=jax.ShapeDtypeStruct(h.shape, h.dtype),
        scratch_shapes=[pltpu.VMEM((hist + SEQ_TILE, d), _F32),
                        pltpu.VMEM((SEQ_TILE, d), _BF16)],
        compiler_params=_params(2),
        name="conformer",
    )(h, g, w1, b1, dw, bdw, lng, lnb, w2, b2)


def _row(v):
    return v.reshape(1, -1)


def kernel(x, ln1_0, a0_w_in, a0_conv, a0_w_out, ln2_0, ffn0_w_gu, ffn0_w_down, ln1_1, b1_w_grp, b1_scale, ln2_1, ffn1_w_gu, ffn1_w_down, ln1_2, c2_w_pw1, c2_b_pw1, c2_dw, c2_b_dw, c2_ln_g, c2_ln_b, c2_w_pw2, c2_b_pw2, ln2_2, ffn2_w_gu, ffn2_w_down, ln1_3, a3_w_in, a3_conv, a3_w_out, ln2_3, ffn3_w_gu, ffn3_w_down, ln_f):
    bf = lambda w: w.astype(_BF16)
    lnf = _row(ln_f)
    h = x
    h = _short_conv(h, _row(ln1_0), bf(a0_w_in), a0_conv, bf(a0_w_out))
    h = _ffn(h, _row(ln2_0), bf(ffn0_w_gu), bf(ffn0_w_down), lnf, False)
    h = _pool(h, _row(ln1_1), bf(b1_w_grp), _row(b1_scale))
    h = _ffn(h, _row(ln2_1), bf(ffn1_w_gu), bf(ffn1_w_down), lnf, False)
    h = _conformer(h, _row(ln1_2), bf(c2_w_pw1), _row(c2_b_pw1), c2_dw, _row(c2_b_dw),
                   _row(c2_ln_g), _row(c2_ln_b), bf(c2_w_pw2), _row(c2_b_pw2))
    h = _ffn(h, _row(ln2_2), bf(ffn2_w_gu), bf(ffn2_w_down), lnf, False)
    h = _short_conv(h, _row(ln1_3), bf(a3_w_in), a3_conv, bf(a3_w_out))
    h = _ffn(h, _row(ln2_3), bf(ffn3_w_gu), bf(ffn3_w_down), lnf, True)
    return h
```

```python
import functools

import jax
import jax.numpy as jnp
from jax import lax
from jax.experimental import pallas as pl
from jax.experimental.pallas import tpu as pltpu

RMS_EPS = 1e-6
LN_EPS = 1e-5
POOL_WINDOWS = (2, 4, 8, 16)

SEQ_TILE = 512
FFN_CHUNK = 256
SUBLANES = 8
VMEM_LIMIT = 52 * 1024 * 1024

_BF16 = jnp.bfloat16
_F32 = jnp.float32


def _dot(a, b):
    return jnp.dot(a, b, preferred_element_type=_F32)


def _rmsnorm(x, g):
    ms = jnp.mean(x * x, axis=-1, keepdims=True)
    return x * lax.rsqrt(ms + RMS_EPS) * g


def _resident(shape):
    zeros = (0,) * len(shape)
    return pl.BlockSpec(shape, lambda *_: zeros, pipeline_mode=pl.Buffered(1))


def _row_tile(d):
    return pl.BlockSpec((None, SEQ_TILE, d), lambda b, s: (b, s, 0))


def _params(n_axes):
    return pltpu.CompilerParams(
        dimension_semantics=("arbitrary",) * n_axes,
        vmem_limit_bytes=VMEM_LIMIT)


def _swiglu_residual(x, g_ref, wgu_ref, wdown_ref, act_ref):
    f = wdown_ref.shape[0]
    xn = _rmsnorm(x, g_ref[...]).astype(_BF16)
    for j in range(f // FFN_CHUNK):
        lo = j * FFN_CHUNK
        gate = _dot(xn, wgu_ref[:, lo:lo + FFN_CHUNK])
        up = _dot(xn, wgu_ref[:, f + lo:f + lo + FFN_CHUNK])
        act_ref[:, lo:lo + FFN_CHUNK] = (gate * jax.nn.sigmoid(gate) * up).astype(_BF16)
    return x + _dot(act_ref[...], wdown_ref[...])


def _ffn_kernel(h_ref, g_ref, wgu_ref, wdown_ref, lnf_ref, o_ref, act_ref, *, final_norm):
    y = _swiglu_residual(h_ref[...], g_ref, wgu_ref, wdown_ref, act_ref)
    if final_norm:
        y = _rmsnorm(y, lnf_ref[...])
    o_ref[...] = y


def _ffn(h, g, w_gu, w_down, ln_f, final_norm):
    bn, s, d = h.shape
    f = w_down.shape[0]
    return pl.pallas_call(
        functools.partial(_ffn_kernel, final_norm=final_norm),
        grid=(bn, s // SEQ_TILE),
        in_specs=[_row_tile(d), _resident((1, d)), _resident((d, 2 * f)),
                  _resident((f, d)), _resident((1, d))],
        out_specs=_row_tile(d),
        out_shape=jax.ShapeDtypeStruct(h.shape, h.dtype),
        scratch_shapes=[pltpu.VMEM((SEQ_TILE, f), _BF16)],
        compiler_params=_params(2),
        name="ffn_final" if final_norm else "ffn",
    )(h, g, w_gu, w_down, ln_f)


def _short_conv_kernel(h_ref, g_ref, win_ref, cw_ref, wout_ref, o_ref, u_ref):
    d = h_ref.shape[-1]
    hist = SUBLANES

    @pl.when(pl.program_id(1) == 0)
    def _():
        u_ref[0:hist, :] = jnp.zeros((hist, d), _F32)

    x = h_ref[...]
    xn = _rmsnorm(x, g_ref[...]).astype(_BF16)
    bcv = _dot(xn, win_ref[...])
    gate_b = bcv[:, 0:d]
    u_ref[hist:hist + SEQ_TILE, :] = bcv[:, d:2 * d] * bcv[:, 2 * d:3 * d]
    k = cw_ref.shape[0]
    conv = cw_ref[k - 1:k, :] * u_ref[hist:hist + SEQ_TILE, :]
    for j in range(1, k):
        conv = conv + cw_ref[k - 1 - j:k - j, :] * u_ref[hist - j:hist - j + SEQ_TILE, :]
    o_ref[...] = x + _dot((gate_b * conv).astype(_BF16), wout_ref[...])
    u_ref[0:hist, :] = u_ref[SEQ_TILE:SEQ_TILE + hist, :]


def _short_conv(h, g, w_in, conv_w, w_out):
    bn, s, d = h.shape
    return pl.pallas_call(
        _short_conv_kernel,
        grid=(bn, s // SEQ_TILE),
        in_specs=[_row_tile(d), _resident((1, d)), _resident((d, 3 * d)),
                  _resident(conv_w.shape), _resident((d, d))],
        out_specs=_row_tile(d),
        out_shape=jax.ShapeDtypeStruct(h.shape, h.dtype),
        scratch_shapes=[pltpu.VMEM((SUBLANES + SEQ_TILE, d), _F32)],
        compiler_params=_params(2),
        name="short_conv",
    )(h, g, w_in, conv_w, w_out)


def _pool_ffn_kernel(h_ref, g1_ref, wgrp_ref, scale_ref, g2_ref, wgu_ref, wdown_ref,
                     o_ref, xn_ref, act_ref):
    d = h_ref.shape[-1]
    hist = max(POOL_WINDOWS)
    gw = d // len(POOL_WINDOWS)

    @pl.when(pl.program_id(1) == 0)
    def _():
        xn_ref[0:hist, :] = jnp.zeros((hist, d), _F32)

    x = h_ref[...]
    xn_ref[hist:hist + SEQ_TILE, :] = _rmsnorm(x, g1_ref[...])
    t = pl.program_id(1) * SEQ_TILE + lax.broadcasted_iota(jnp.int32, (SEQ_TILE, 1), 0)
    for gi, w in enumerate(POOL_WINDOWS):
        c0 = gi * gw
        cur = xn_ref[hist:hist + SEQ_TILE, c0:c0 + gw]
        tot = cur
        for j in range(1, w):
            tot = tot + xn_ref[hist - j:hist - j + SEQ_TILE, c0:c0 + gw]
        cnt = jnp.minimum(t + 1, w).astype(_F32)
        mixed = (tot / cnt - cur).astype(_BF16)
        y = _dot(mixed, wgrp_ref[gi]) * scale_ref[:, c0:c0 + gw]
        o_ref[:, c0:c0 + gw] = x[:, c0:c0 + gw] + y
    xn_ref[0:hist, :] = xn_ref[SEQ_TILE:SEQ_TILE + hist, :]
    o_ref[...] = _swiglu_residual(o_ref[...], g2_ref, wgu_ref, wdown_ref, act_ref)


def _pool_ffn(h, g1, w_grp, scale, g2, w_gu, w_down):
    bn, s, d = h.shape
    f = w_down.shape[0]
    hist = max(POOL_WINDOWS)
    return pl.pallas_call(
        _pool_ffn_kernel,
        grid=(bn, s // SEQ_TILE),
        in_specs=[_row_tile(d), _resident((1, d)), _resident(w_grp.shape), _resident((1, d)),
                  _resident((1, d)), _resident((d, 2 * f)), _resident((f, d))],
        out_specs=_row_tile(d),
        out_shape=jax.ShapeDtypeStruct(h.shape, h.dtype),
        scratch_shapes=[pltpu.VMEM((hist + SEQ_TILE, d), _F32),
                        pltpu.VMEM((SEQ_TILE, f), _BF16)],
        compiler_params=_params(2),
        name="pool_ffn",
    )(h, g1, w_grp, scale, g2, w_gu, w_down)


CONF_ROWS = 32


def _conformer_kernel(h_ref, g_ref, w1_ref, b1_ref, dw_ref, bdw_ref, lng_ref, lnb_ref,
                      w2_ref, b2_ref, o_ref, glu_ref, shift_ref, dwb_ref, z_ref):
    d = h_ref.shape[-1]
    k = dw_ref.shape[0]
    n = glu_ref.shape[0]
    hist = n - SEQ_TILE

    @pl.when((pl.program_id(0) == 0) & (pl.program_id(1) == 0))
    def _():
        for j in range(k):
            dwb_ref[j] = jnp.broadcast_to(dw_ref[j:j + 1, :], (SUBLANES, d))

    @pl.when(pl.program_id(1) == 0)
    def _():
        glu_ref[0:hist, :] = jnp.zeros((hist, d), _F32)

    x = h_ref[...]
    xn = _rmsnorm(x, g_ref[...]).astype(_BF16)
    ag = _dot(xn, w1_ref[...]) + b1_ref[...]
    glu_ref[hist:hist + SEQ_TILE, :] = ag[:, 0:d] * jax.nn.sigmoid(ag[:, d:2 * d])

    for r in range(1, SUBLANES):
        rolled = pltpu.roll(glu_ref[...], n - r, axis=0)
        shift_ref[r - 1] = rolled[0:n - SUBLANES, :]

    groups = CONF_ROWS // SUBLANES
    for r0 in range(0, SEQ_TILE, CONF_ROWS):
        acc = jnp.zeros((groups, SUBLANES, d), _F32) + bdw_ref[...]
        for j in range(k):
            off = hist - (k - 1) + j
            r = off % SUBLANES
            base = r0 + off - r
            if r == 0:
                win = glu_ref[base:base + CONF_ROWS, :]
            else:
                win = shift_ref[r - 1, base:base + CONF_ROWS, :]
            acc = acc + dwb_ref[j] * win.reshape(groups, SUBLANES, d)
        acc = acc.reshape(CONF_ROWS, d)
        mu = jnp.mean(acc, axis=-1, keepdims=True)
        xc = acc - mu
        var = jnp.mean(xc * xc, axis=-1, keepdims=True)
        ln = xc * lax.rsqrt(var + LN_EPS) * lng_ref[...] + lnb_ref[...]
        z_ref[r0:r0 + CONF_ROWS, :] = (ln * jax.nn.sigmoid(ln)).astype(_BF16)
    o_ref[...] = x + _dot(z_ref[...], w2_ref[...]) + b2_ref[...]
    glu_ref[0:hist, :] = glu_ref[SEQ_TILE:SEQ_TILE + hist, :]


def _conformer(h, g, w1, b1, dw, bdw, lng, lnb, w2, b2):
    bn, s, d = h.shape
    hist = -(-(dw.shape[0] - 1) // SUBLANES) * SUBLANES
    return pl.pallas_call(
        _conformer_kernel,
        grid=(bn, s // SEQ_TILE),
        in_specs=[_row_tile(d), _resident((1, d)), _resident((d, 2 * d)), _resident((1, 2 * d)),
                  _resident(dw.shape), _resident((1, d)), _resident((1, d)), _resident((1, d)),
                  _resident((d, d)), _resident((1, d))],
        out_specs=_row_tile(d),
        out_shape=jax.ShapeDtypeStruct(h.shape, h.dtype),
        scratch_shapes=[pltpu.VMEM((hist + SEQ_TILE, d), _F32),
                        pltpu.VMEM((SUBLANES - 1, hist + SEQ_TILE - SUBLANES, d), _F32),
                        pltpu.VMEM((dw.shape[0], SUBLANES, d), _F32),
                        pltpu.VMEM((SEQ_TILE, d), _BF16)],
        compiler_params=_params(2),
        name="conformer",
    )(h, g, w1, b1, dw, bdw, lng, lnb, w2, b2)


def _row(v):
    return v.reshape(1, -1)


def kernel(x, ln1_0, a0_w_in, a0_conv, a0_w_out, ln2_0, ffn0_w_gu, ffn0_w_down, ln1_1, b1_w_grp, b1_scale, ln2_1, ffn1_w_gu, ffn1_w_down, ln1_2, c2_w_pw1, c2_b_pw1, c2_dw, c2_b_dw, c2_ln_g, c2_ln_b, c2_w_pw2, c2_b_pw2, ln2_2, ffn2_w_gu, ffn2_w_down, ln1_3, a3_w_in, a3_conv, a3_w_out, ln2_3, ffn3_w_gu, ffn3_w_down, ln_f):
    bf = lambda w: w.astype(_BF16)
    lnf = _row(ln_f)
    h = x
    h = _short_conv(h, _row(ln1_0), bf(a0_w_in), a0_conv, bf(a0_w_out))
    h = _ffn(h, _row(ln2_0), bf(ffn0_w_gu), bf(ffn0_w_down), lnf, False)
    h = _pool_ffn(h, _row(ln1_1), bf(b1_w_grp), _row(b1_scale),
                  _row(ln2_1), bf(ffn1_w_gu), bf(ffn1_w_down))
    h = _conformer(h, _row(ln1_2), bf(c2_w_pw1), _row(c2_b_pw1), c2_dw, _row(c2_b_dw),
                   _row(c2_ln_g), _row(c2_ln_b), bf(c2_w_pw2), _row(c2_b_pw2))
    h = _ffn(h, _row(ln2_2), bf(ffn2_w_gu), bf(ffn2_w_down), lnf, False)
    h = _short_conv(h, _row(ln1_3), bf(a3_w_in), a3_conv, bf(a3_w_out))
    h = _ffn(h, _row(ln2_3), bf(ffn3_w_gu), bf(ffn3_w_down), lnf, True)
    return h
```

```python
import math

import jax
import jax.numpy as jnp
from jax import lax
from jax.experimental import pallas as pl
from jax.experimental.pallas import tpu as pltpu

RMS_EPS = 1e-6
LN_EPS = 1e-5
POOL_WINDOWS = (2, 4, 8, 16)

SEQ_TILE = 512
FFN_TILE = 1024
FFN_CHUNK = 256
SUBLANES = 8
BF16_ROWS = 16
VMEM_LIMIT = 58 * 1024 * 1024

_BF16 = jnp.bfloat16
_F32 = jnp.float32


def _dot(a, b):
    return jnp.dot(a, b, preferred_element_type=_F32)


def _rmsnorm(x, g):
    ms = jnp.mean(x * x, axis=-1, keepdims=True)
    return x * lax.rsqrt(ms + RMS_EPS) * g


def _resident(shape):
    zeros = (0,) * len(shape)
    return pl.BlockSpec(shape, lambda *_: zeros, pipeline_mode=pl.Buffered(1))


def _cast_blocks(shape, steps):
    if len(shape) > 2:
        return math.gcd(shape[0], steps)
    nb = steps
    while shape[0] % (nb * BF16_ROWS):
        nb //= 2
    return nb


def _tile_call(body, name, h, residents, scratch_shapes, casts=(), tile=SEQ_TILE):
    bn, s, d = h.shape
    tiles = s // tile
    steps = bn * tiles
    n_in = 1 + len(residents)
    n_cast = len(casts)
    row_tile = pl.BlockSpec((None, tile, d), lambda b, t: (b, t, 0))

    cast_specs = []
    for w in casts:
        nb = _cast_blocks(w.shape, steps)
        tail = (0,) * (w.ndim - 1)
        cast_specs.append(pl.BlockSpec(
            (w.shape[0] // nb,) + w.shape[1:],
            lambda b, t, nb=nb, tail=tail: (((b * tiles + t) * nb) // steps,) + tail))

    def wrapped(*refs):
        srcs = refs[n_in:n_in + n_cast]
        dsts = refs[n_in + n_cast + 1:n_in + 2 * n_cast + 1]
        for src, dst in zip(srcs, dsts):
            dst[...] = src[...].astype(_BF16)
        body(*refs[:n_in], refs[n_in + n_cast], *refs[n_in + 2 * n_cast + 1:])

    outs = pl.pallas_call(
        wrapped,
        grid=(bn, tiles),
        in_specs=[row_tile] + [_resident(r.shape) for r in residents] + cast_specs,
        out_specs=[row_tile] + cast_specs,
        out_shape=[jax.ShapeDtypeStruct(h.shape, h.dtype)]
        + [jax.ShapeDtypeStruct(w.shape, _BF16) for w in casts],
        scratch_shapes=scratch_shapes,
        compiler_params=pltpu.CompilerParams(
            dimension_semantics=("arbitrary", "arbitrary"),
            vmem_limit_bytes=VMEM_LIMIT),
        name=name,
    )(h, *residents, *casts)
    return outs[0], outs[1:]


def _swiglu_residual(x, g_ref, wgu_ref, wdown_ref, act_ref):
    f = wdown_ref.shape[0]
    xn = _rmsnorm(x, g_ref[...]).astype(_BF16)
    for j in range(f // FFN_CHUNK):
        lo = j * FFN_CHUNK
        gate = _dot(xn, wgu_ref[:, lo:lo + FFN_CHUNK])
        up = _dot(xn, wgu_ref[:, f + lo:f + lo + FFN_CHUNK])
        act_ref[:, lo:lo + FFN_CHUNK] = (gate * jax.nn.sigmoid(gate) * up).astype(_BF16)
    return x + _dot(act_ref[...], wdown_ref[...])


def _ffn_kernel(h_ref, g_ref, wgu_ref, wdown_ref, o_ref, act_ref):
    o_ref[...] = _swiglu_residual(h_ref[...], g_ref, wgu_ref, wdown_ref, act_ref)


def _ffn_final_kernel(h_ref, g_ref, wgu_ref, wdown_ref, lnf_ref, o_ref, act_ref):
    y = _swiglu_residual(h_ref[...], g_ref, wgu_ref, wdown_ref, act_ref)
    o_ref[...] = _rmsnorm(y, lnf_ref[...])


def _act_scratch(w_down, tile=SEQ_TILE):
    return pltpu.VMEM((tile, w_down.shape[0]), _BF16)


def _ffn(h, g, w_gu, w_down, casts):
    return _tile_call(_ffn_kernel, "ffn", h, [g, w_gu, w_down],
                      [_act_scratch(w_down, FFN_TILE)], casts, tile=FFN_TILE)


def _ffn_final(h, g, w_gu, w_down, ln_f):
    return _tile_call(_ffn_final_kernel, "ffn_final", h, [g, w_gu, w_down, ln_f],
                      [_act_scratch(w_down, FFN_TILE)], tile=FFN_TILE)[0]


def _short_conv_kernel(h_ref, g_ref, win_ref, cw_ref, wout_ref, o_ref, u_ref):
    d = h_ref.shape[-1]
    hist = SUBLANES

    @pl.when(pl.program_id(1) == 0)
    def _():
        u_ref[0:hist, :] = jnp.zeros((hist, d), _F32)

    x = h_ref[...]
    xn = _rmsnorm(x, g_ref[...]).astype(_BF16)
    bcv = _dot(xn, win_ref[...])
    gate_b = bcv[:, 0:d]
    u_ref[hist:hist + SEQ_TILE, :] = bcv[:, d:2 * d] * bcv[:, 2 * d:3 * d]
    k = cw_ref.shape[0]
    conv = cw_ref[k - 1:k, :] * u_ref[hist:hist + SEQ_TILE, :]
    for j in range(1, k):
        conv = conv + cw_ref[k - 1 - j:k - j, :] * u_ref[hist - j:hist - j + SEQ_TILE, :]
    o_ref[...] = x + _dot((gate_b * conv).astype(_BF16), wout_ref[...])
    u_ref[0:hist, :] = u_ref[SEQ_TILE:SEQ_TILE + hist, :]


def _short_conv(h, g, w_in, conv_w, w_out, casts):
    d = h.shape[-1]
    return _tile_call(_short_conv_kernel, "short_conv", h, [g, w_in, conv_w, w_out],
                      [pltpu.VMEM((SUBLANES + SEQ_TILE, d), _F32)], casts)


def _pool_ffn_kernel(h_ref, g1_ref, wgrp_ref, scale_ref, g2_ref, wgu_ref, wdown_ref,
                     o_ref, xn_ref, act_ref):
    d = h_ref.shape[-1]
    hist = max(POOL_WINDOWS)
    gw = d // len(POOL_WINDOWS)

    @pl.when(pl.program_id(1) == 0)
    def _():
        xn_ref[0:hist, :] = jnp.zeros((hist, d), _F32)

    x = h_ref[...]
    xn_ref[hist:hist + SEQ_TILE, :] = _rmsnorm(x, g1_ref[...])
    t = pl.program_id(1) * SEQ_TILE + lax.broadcasted_iota(jnp.int32, (SEQ_TILE, 1), 0)
    for gi, w in enumerate(POOL_WINDOWS):
        c0 = gi * gw
        cur = xn_ref[hist:hist + SEQ_TILE, c0:c0 + gw]
        tot = cur
        for j in range(1, w):
            tot = tot + xn_ref[hist - j:hist - j + SEQ_TILE, c0:c0 + gw]
        cnt = jnp.minimum(t + 1, w).astype(_F32)
        mixed = (tot / cnt - cur).astype(_BF16)
        y = _dot(mixed, wgrp_ref[gi]) * scale_ref[:, c0:c0 + gw]
        o_ref[:, c0:c0 + gw] = x[:, c0:c0 + gw] + y
    xn_ref[0:hist, :] = xn_ref[SEQ_TILE:SEQ_TILE + hist, :]
    o_ref[...] = _swiglu_residual(o_ref[...], g2_ref, wgu_ref, wdown_ref, act_ref)


def _pool_ffn(h, g1, w_grp, scale, g2, w_gu, w_down, casts):
    d = h.shape[-1]
    return _tile_call(_pool_ffn_kernel, "pool_ffn", h, [g1, w_grp, scale, g2, w_gu, w_down],
                      [pltpu.VMEM((max(POOL_WINDOWS) + SEQ_TILE, d), _F32), _act_scratch(w_down)],
                      casts)


CONF_ROWS = 32


def _conformer_kernel(h_ref, g_ref, w1_ref, b1_ref, dw_ref, bdw_ref, lng_ref, lnb_ref,
                      w2_ref, b2_ref, o_ref, glu_ref, shift_ref, dwb_ref, z_ref):
    d = h_ref.shape[-1]
    k = dw_ref.shape[0]
    n = glu_ref.shape[0]
    hist = n - SEQ_TILE

    @pl.when((pl.program_id(0) == 0) & (pl.program_id(1) == 0))
    def _():
        for j in range(k):
            dwb_ref[j] = jnp.broadcast_to(dw_ref[j:j + 1, :], (SUBLANES, d))

    @pl.when(pl.program_id(1) == 0)
    def _():
        glu_ref[0:hist, :] = jnp.zeros((hist, d), _F32)

    x = h_ref[...]
    xn = _rmsnorm(x, g_ref[...]).astype(_BF16)
    ag = _dot(xn, w1_ref[...]) + b1_ref[...]
    glu_ref[hist:hist + SEQ_TILE, :] = ag[:, 0:d] * jax.nn.sigmoid(ag[:, d:2 * d])

    for r in range(1, SUBLANES):
        rolled = pltpu.roll(glu_ref[...], n - r, axis=0)
        shift_ref[r - 1] = rolled[0:n - SUBLANES, :]

    groups = CONF_ROWS // SUBLANES
    for r0 in range(0, SEQ_TILE, CONF_ROWS):
        acc = jnp.zeros((groups, SUBLANES, d), _F32) + bdw_ref[...]
        for j in range(k):
            off = hist - (k - 1) + j
            r = off % SUBLANES
            base = r0 + off - r
            if r == 0:
                win = glu_ref[base:base + CONF_ROWS, :]
            else:
                win = shift_ref[r - 1, base:base + CONF_ROWS, :]
            acc = acc + dwb_ref[j] * win.reshape(groups, SUBLANES, d)
        acc = acc.reshape(CONF_ROWS, d)
        mu = jnp.mean(acc, axis=-1, keepdims=True)
        xc = acc - mu
        var = jnp.mean(xc * xc, axis=-1, keepdims=True)
        ln = xc * lax.rsqrt(var + LN_EPS) * lng_ref[...] + lnb_ref[...]
        z_ref[r0:r0 + CONF_ROWS, :] = (ln * jax.nn.sigmoid(ln)).astype(_BF16)
    o_ref[...] = x + _dot(z_ref[...], w2_ref[...]) + b2_ref[...]
    glu_ref[0:hist, :] = glu_ref[SEQ_TILE:SEQ_TILE + hist, :]


def _conformer(h, g, w1, b1, dw, bdw, lng, lnb, w2, b2, casts):
    d = h.shape[-1]
    hist = -(-(dw.shape[0] - 1) // SUBLANES) * SUBLANES
    return _tile_call(
        _conformer_kernel, "conformer", h, [g, w1, b1, dw, bdw, lng, lnb, w2, b2],
        [pltpu.VMEM((hist + SEQ_TILE, d), _F32),
         pltpu.VMEM((SUBLANES - 1, hist + SEQ_TILE - SUBLANES, d), _F32),
         pltpu.VMEM((dw.shape[0], SUBLANES, d), _F32),
         pltpu.VMEM((SEQ_TILE, d), _BF16)],
        casts)


def _row(v):
    return v.reshape(1, -1)


def kernel(x, ln1_0, a0_w_in, a0_conv, a0_w_out, ln2_0, ffn0_w_gu, ffn0_w_down, ln1_1, b1_w_grp, b1_scale, ln2_1, ffn1_w_gu, ffn1_w_down, ln1_2, c2_w_pw1, c2_b_pw1, c2_dw, c2_b_dw, c2_ln_g, c2_ln_b, c2_w_pw2, c2_b_pw2, ln2_2, ffn2_w_gu, ffn2_w_down, ln1_3, a3_w_in, a3_conv, a3_w_out, ln2_3, ffn3_w_gu, ffn3_w_down, ln_f):
    h, (wgu0, wdown0) = _short_conv(
        x, _row(ln1_0), a0_w_in.astype(_BF16), a0_conv, a0_w_out.astype(_BF16),
        casts=(ffn0_w_gu, ffn0_w_down))
    h, (wgrp, wgu1, wdown1) = _ffn(
        h, _row(ln2_0), wgu0, wdown0, casts=(b1_w_grp, ffn1_w_gu, ffn1_w_down))
    h, (w_pw1, w_pw2) = _pool_ffn(
        h, _row(ln1_1), wgrp, _row(b1_scale), _row(ln2_1), wgu1, wdown1,
        casts=(c2_w_pw1, c2_w_pw2))
    h, (wgu2, wdown2) = _conformer(
        h, _row(ln1_2), w_pw1, _row(c2_b_pw1), c2_dw, _row(c2_b_dw), _row(c2_ln_g),
        _row(c2_ln_b), w_pw2, _row(c2_b_pw2), casts=(ffn2_w_gu, ffn2_w_down))
    h, (w_in3, w_out3) = _ffn(h, _row(ln2_2), wgu2, wdown2, casts=(a3_w_in, a3_w_out))
    h, (wgu3, wdown3) = _short_conv(
        h, _row(ln1_3), w_in3, a3_conv, w_out3, casts=(ffn3_w_gu, ffn3_w_down))
    return _ffn_final(h, _row(ln2_3), wgu3, wdown3, _row(ln_f))
```

```python
import math

import jax
import jax.numpy as jnp
from jax import lax
from jax.experimental import pallas as pl
from jax.experimental.pallas import tpu as pltpu

RMS_EPS = 1e-6
LN_EPS = 1e-5
POOL_WINDOWS = (2, 4, 8, 16)

SEQ_TILE = 512
FFN_TILE = 1024
FFN_CHUNK = 256
MXU_COLS = 256
SUBLANES = 8
BF16_ROWS = 16
VMEM_LIMIT = 58 * 1024 * 1024

_BF16 = jnp.bfloat16
_F32 = jnp.float32


def _dot(a, b):
    return jnp.dot(a, b, preferred_element_type=_F32)


def _rmsnorm(x, g):
    ms = jnp.mean(x * x, axis=-1, keepdims=True)
    return x * lax.rsqrt(ms + RMS_EPS) * g


def _resident(shape):
    zeros = (0,) * len(shape)
    return pl.BlockSpec(shape, lambda *_: zeros, pipeline_mode=pl.Buffered(1))


def _cast_blocks(shape, steps):
    if len(shape) > 2:
        return math.gcd(shape[0], steps)
    nb = steps
    while shape[0] % (nb * BF16_ROWS):
        nb //= 2
    return nb


def _tile_call(body, name, h, residents, scratch_shapes, casts=(), tile=SEQ_TILE):
    bn, s, d = h.shape
    tiles = s // tile
    steps = bn * tiles
    n_in = 1 + len(residents)
    n_cast = len(casts)
    row_tile = pl.BlockSpec((None, tile, d), lambda b, t: (b, t, 0))

    cast_specs = []
    for w in casts:
        nb = _cast_blocks(w.shape, steps)
        tail = (0,) * (w.ndim - 1)
        cast_specs.append(pl.BlockSpec(
            (w.shape[0] // nb,) + w.shape[1:],
            lambda b, t, nb=nb, tail=tail: (((b * tiles + t) * nb) // steps,) + tail))

    def wrapped(*refs):
        srcs = refs[n_in:n_in + n_cast]
        dsts = refs[n_in + n_cast + 1:n_in + 2 * n_cast + 1]
        for src, dst in zip(srcs, dsts):
            dst[...] = src[...].astype(_BF16)
        body(*refs[:n_in], refs[n_in + n_cast], *refs[n_in + 2 * n_cast + 1:])

    outs = pl.pallas_call(
        wrapped,
        grid=(bn, tiles),
        in_specs=[row_tile] + [_resident(r.shape) for r in residents] + cast_specs,
        out_specs=[row_tile] + cast_specs,
        out_shape=[jax.ShapeDtypeStruct(h.shape, h.dtype)]
        + [jax.ShapeDtypeStruct(w.shape, _BF16) for w in casts],
        scratch_shapes=scratch_shapes,
        compiler_params=pltpu.CompilerParams(
            dimension_semantics=("arbitrary", "arbitrary"),
            vmem_limit_bytes=VMEM_LIMIT),
        name=name,
    )(h, *residents, *casts)
    return outs[0], outs[1:]


def _alternate(a, n_a, b, n_b):
    i_a = i_b = 0
    while i_a < n_a or i_b < n_b:
        if i_b >= n_b or (i_a < n_a and i_a * n_b <= i_b * n_a):
            next(a)
            i_a += 1
        else:
            next(b)
            i_b += 1


def _layer_call(prepare, stage, pieces, head_pieces, name, h, residents, scratch_shapes, ffn,
                casts=()):
    bn, s, d = h.shape
    per_seq = s // SEQ_TILE
    tiles = bn * per_seq
    n_mix = 1 + len(residents)
    n_in = n_mix + len(ffn)
    n_cast = len(casts)

    def tile_spec(tile_of_step):
        def index(t):
            tile = tile_of_step(t)
            return tile // per_seq, tile % per_seq, 0
        return pl.BlockSpec((None, SEQ_TILE, d), index)

    mixer_tile = lambda t: jnp.minimum(t, tiles - 1)
    cast_specs = []
    for w in casts:
        nb = _cast_blocks(w.shape, tiles)
        tail = (0,) * (w.ndim - 1)
        cast_specs.append(pl.BlockSpec(
            (w.shape[0] // nb,) + w.shape[1:],
            lambda t, nb=nb, tail=tail: ((mixer_tile(t) * nb) // tiles,) + tail))

    def wrapped(*refs):
        srcs = refs[n_in:n_in + n_cast]
        o_ref = refs[n_in + n_cast]
        dsts = refs[n_in + n_cast + 1:n_in + 2 * n_cast + 1]
        mid_ref, act_ref = refs[n_in + 2 * n_cast + 1:n_in + 2 * n_cast + 3]
        t = pl.program_id(0)
        for src, dst in zip(srcs, dsts):
            dst[...] = src[...].astype(_BF16)

        @pl.when(t == 0)
        def _():
            mid_ref[...] = jnp.zeros(mid_ref.shape, _F32)

        stage_args = refs[:n_mix] + (mid_ref,) + refs[n_in + 2 * n_cast + 3:]
        seq_tile = mixer_tile(t) % per_seq
        prepare(*stage_args, first_step=t == 0, seq_tile=seq_tile)
        mixer = stage(*stage_args, seq_tile=seq_tile)
        for _ in range(head_pieces):
            next(mixer)
        swiglu = _swiglu_pieces(mid_ref, *refs[n_mix:n_in], act_ref, o_ref)
        _alternate(swiglu, _swiglu_piece_count(ffn[2].shape), mixer, pieces - head_pieces)

    w_down = ffn[2]
    outs = pl.pallas_call(
        wrapped,
        grid=(tiles + 1,),
        in_specs=[tile_spec(mixer_tile)] + [_resident(r.shape) for r in residents + list(ffn)]
        + cast_specs,
        out_specs=[tile_spec(lambda t: jnp.maximum(t - 1, 0))] + cast_specs,
        out_shape=[jax.ShapeDtypeStruct(h.shape, h.dtype)]
        + [jax.ShapeDtypeStruct(w.shape, _BF16) for w in casts],
        scratch_shapes=[pltpu.VMEM((SEQ_TILE, d), _F32), _act_scratch(w_down)] + scratch_shapes,
        compiler_params=pltpu.CompilerParams(
            dimension_semantics=("arbitrary",),
            vmem_limit_bytes=VMEM_LIMIT),
        name=name,
    )(h, *residents, *ffn, *casts)
    return outs[0], outs[1:]


def _swiglu(x, g_ref, wgu_ref, wdown_ref, act_ref):
    f = wdown_ref.shape[0]
    xn = _rmsnorm(x, g_ref[...]).astype(_BF16)
    for j in range(f // FFN_CHUNK):
        lo = j * FFN_CHUNK
        gate = _dot(xn, wgu_ref[:, lo:lo + FFN_CHUNK])
        up = _dot(xn, wgu_ref[:, f + lo:f + lo + FFN_CHUNK])
        act_ref[:, lo:lo + FFN_CHUNK] = (gate * jax.nn.sigmoid(gate) * up).astype(_BF16)
    return _dot(act_ref[...], wdown_ref[...])


def _swiglu_piece_count(w_down_shape):
    return 1 + w_down_shape[0] // FFN_CHUNK + w_down_shape[1] // MXU_COLS


def _swiglu_pieces(x_ref, g_ref, wgu_ref, wdown_ref, act_ref, o_ref):
    f, d = wdown_ref.shape
    x = x_ref[...]
    o_ref[...] = x
    xn = _rmsnorm(x, g_ref[...]).astype(_BF16)
    yield
    for j in range(f // FFN_CHUNK):
        lo = j * FFN_CHUNK
        gate = _dot(xn, wgu_ref[:, lo:lo + FFN_CHUNK])
        up = _dot(xn, wgu_ref[:, f + lo:f + lo + FFN_CHUNK])
        act_ref[:, lo:lo + FFN_CHUNK] = (gate * jax.nn.sigmoid(gate) * up).astype(_BF16)
        yield
    for c0 in range(0, d, MXU_COLS):
        o_ref[:, c0:c0 + MXU_COLS] += _dot(act_ref[...], wdown_ref[:, c0:c0 + MXU_COLS])
        yield


def _ffn_kernel(h_ref, g_ref, wgu_ref, wdown_ref, o_ref, act_ref):
    x = h_ref[...]
    o_ref[...] = x + _swiglu(x, g_ref, wgu_ref, wdown_ref, act_ref)


def _ffn_final_kernel(h_ref, g_ref, wgu_ref, wdown_ref, lnf_ref, o_ref, act_ref):
    x = h_ref[...]
    y = x + _swiglu(x, g_ref, wgu_ref, wdown_ref, act_ref)
    o_ref[...] = _rmsnorm(y, lnf_ref[...])


def _act_scratch(w_down, tile=SEQ_TILE):
    return pltpu.VMEM((tile, w_down.shape[0]), _BF16)


def _ffn(h, g, w_gu, w_down, casts):
    return _tile_call(_ffn_kernel, "ffn", h, [g, w_gu, w_down],
                      [_act_scratch(w_down, FFN_TILE)], casts, tile=FFN_TILE)


def _ffn_final(h, g, w_gu, w_down, ln_f):
    return _tile_call(_ffn_final_kernel, "ffn_final", h, [g, w_gu, w_down, ln_f],
                      [_act_scratch(w_down, FFN_TILE)], tile=FFN_TILE)[0]


def _short_conv_kernel(h_ref, g_ref, win_ref, cw_ref, wout_ref, o_ref, u_ref):
    d = h_ref.shape[-1]
    hist = SUBLANES

    @pl.when(pl.program_id(1) == 0)
    def _():
        u_ref[0:hist, :] = jnp.zeros((hist, d), _F32)

    x = h_ref[...]
    xn = _rmsnorm(x, g_ref[...]).astype(_BF16)
    bcv = _dot(xn, win_ref[...])
    gate_b = bcv[:, 0:d]
    u_ref[hist:hist + SEQ_TILE, :] = bcv[:, d:2 * d] * bcv[:, 2 * d:3 * d]
    k = cw_ref.shape[0]
    conv = cw_ref[k - 1:k, :] * u_ref[hist:hist + SEQ_TILE, :]
    for j in range(1, k):
        conv = conv + cw_ref[k - 1 - j:k - j, :] * u_ref[hist - j:hist - j + SEQ_TILE, :]
    o_ref[...] = x + _dot((gate_b * conv).astype(_BF16), wout_ref[...])
    u_ref[0:hist, :] = u_ref[SEQ_TILE:SEQ_TILE + hist, :]


def _short_conv(h, g, w_in, conv_w, w_out, casts):
    d = h.shape[-1]
    return _tile_call(_short_conv_kernel, "short_conv", h, [g, w_in, conv_w, w_out],
                      [pltpu.VMEM((SUBLANES + SEQ_TILE, d), _F32)], casts)


def _pool_prepare(h_ref, g_ref, wgrp_ref, scale_ref, dst_ref, xn_ref, *, first_step, seq_tile):
    @pl.when(seq_tile == 0)
    def _():
        xn_ref[0:max(POOL_WINDOWS), :] = jnp.zeros((max(POOL_WINDOWS), xn_ref.shape[1]), _F32)


def _pool_stage(h_ref, g_ref, wgrp_ref, scale_ref, dst_ref, xn_ref, *, seq_tile):
    d = h_ref.shape[-1]
    hist = max(POOL_WINDOWS)
    gw = d // len(POOL_WINDOWS)
    xn_ref[hist:hist + SEQ_TILE, :] = _rmsnorm(h_ref[...], g_ref[...])
    yield
    t = seq_tile * SEQ_TILE + lax.broadcasted_iota(jnp.int32, (SEQ_TILE, 1), 0)
    for gi, w in enumerate(POOL_WINDOWS):
        c0 = gi * gw
        cur = xn_ref[hist:hist + SEQ_TILE, c0:c0 + gw]
        tot = cur
        for j in range(1, w):
            tot = tot + xn_ref[hist - j:hist - j + SEQ_TILE, c0:c0 + gw]
        cnt = jnp.minimum(t + 1, w).astype(_F32)
        mixed = (tot / cnt - cur).astype(_BF16)
        y = _dot(mixed, wgrp_ref[gi]) * scale_ref[:, c0:c0 + gw]
        dst_ref[:, c0:c0 + gw] = h_ref[:, c0:c0 + gw] + y
        if gi == len(POOL_WINDOWS) - 1:
            xn_ref[0:hist, :] = xn_ref[SEQ_TILE:SEQ_TILE + hist, :]
        yield


def _pool_ffn(h, g1, w_grp, scale, g2, w_gu, w_down, casts):
    d = h.shape[-1]
    return _layer_call(_pool_prepare, _pool_stage, 1 + len(POOL_WINDOWS), 0, "pool_ffn", h,
                       [g1, w_grp, scale],
                       [pltpu.VMEM((max(POOL_WINDOWS) + SEQ_TILE, d), _F32)],
                       (g2, w_gu, w_down), casts)


CONF_ROWS = 32
CONV_BLOCK = 128


def _conformer_prepare(h_ref, g_ref, w1_ref, b1_ref, dw_ref, bdw_ref, lng_ref, lnb_ref,
                       w2_ref, b2_ref, dst_ref, glu_ref, shift_ref, dwb_ref, z_ref,
                       *, first_step, seq_tile):
    d = h_ref.shape[-1]
    hist = glu_ref.shape[0] - SEQ_TILE

    @pl.when(first_step)
    def _():
        for j in range(dw_ref.shape[0]):
            dwb_ref[j] = jnp.broadcast_to(dw_ref[j:j + 1, :], (SUBLANES, d))

    @pl.when(seq_tile == 0)
    def _():
        glu_ref[0:hist, :] = jnp.zeros((hist, d), _F32)


def _conformer_stage(h_ref, g_ref, w1_ref, b1_ref, dw_ref, bdw_ref, lng_ref, lnb_ref,
                     w2_ref, b2_ref, dst_ref, glu_ref, shift_ref, dwb_ref, z_ref, *, seq_tile):
    del seq_tile
    d = h_ref.shape[-1]
    k = dw_ref.shape[0]
    hist = glu_ref.shape[0] - SEQ_TILE
    xn = _rmsnorm(h_ref[...], g_ref[...]).astype(_BF16)
    yield
    for c0 in range(0, d, MXU_COLS):
        a = _dot(xn, w1_ref[:, c0:c0 + MXU_COLS]) + b1_ref[:, c0:c0 + MXU_COLS]
        gate = _dot(xn, w1_ref[:, d + c0:d + c0 + MXU_COLS]) + b1_ref[:, d + c0:d + c0 + MXU_COLS]
        glu_ref[hist:hist + SEQ_TILE, c0:c0 + MXU_COLS] = a * jax.nn.sigmoid(gate)
        yield

    groups = CONF_ROWS // SUBLANES
    n = CONV_BLOCK + hist
    for b0 in range(0, SEQ_TILE, CONV_BLOCK):
        for r in range(1, SUBLANES):
            rolled = pltpu.roll(glu_ref[b0:b0 + n, :], n - r, axis=0)
            shift_ref[r - 1] = rolled[0:n - SUBLANES, :]
        yield
        for r0 in range(0, CONV_BLOCK, CONF_ROWS):
            rows = slice(b0 + r0, b0 + r0 + CONF_ROWS)
            acc = jnp.zeros((groups, SUBLANES, d), _F32) + bdw_ref[...]
            for j in range(k):
                off = hist - (k - 1) + j
                r = off % SUBLANES
                base = r0 + off - r
                if r == 0:
                    win = glu_ref[b0 + base:b0 + base + CONF_ROWS, :]
                else:
                    win = shift_ref[r - 1, base:base + CONF_ROWS, :]
                acc = acc + dwb_ref[j] * win.reshape(groups, SUBLANES, d)
            acc = acc.reshape(CONF_ROWS, d)
            mu = jnp.mean(acc, axis=-1, keepdims=True)
            xc = acc - mu
            var = jnp.mean(xc * xc, axis=-1, keepdims=True)
            ln = xc * lax.rsqrt(var + LN_EPS) * lng_ref[...] + lnb_ref[...]
            z_ref[rows, :] = (ln * jax.nn.sigmoid(ln)).astype(_BF16)
            yield
    glu_ref[0:hist, :] = glu_ref[SEQ_TILE:SEQ_TILE + hist, :]
    for c0 in range(0, d, MXU_COLS):
        y = _dot(z_ref[...], w2_ref[:, c0:c0 + MXU_COLS]) + b2_ref[:, c0:c0 + MXU_COLS]
        dst_ref[:, c0:c0 + MXU_COLS] = h_ref[:, c0:c0 + MXU_COLS] + y
        yield


def _conformer_kernel(*refs):
    first_step = (pl.program_id(0) == 0) & (pl.program_id(1) == 0)
    _conformer_prepare(*refs, first_step=first_step, seq_tile=pl.program_id(1))
    for _ in _conformer_stage(*refs, seq_tile=pl.program_id(1)):
        pass


def _conformer(h, g, w1, b1, dw, bdw, lng, lnb, w2, b2, casts):
    d = h.shape[-1]
    hist = -(-(dw.shape[0] - 1) // SUBLANES) * SUBLANES
    return _tile_call(
        _conformer_kernel, "conformer", h, [g, w1, b1, dw, bdw, lng, lnb, w2, b2],
        [pltpu.VMEM((hist + SEQ_TILE, d), _F32),
         pltpu.VMEM((SUBLANES - 1, hist + CONV_BLOCK - SUBLANES, d), _F32),
         pltpu.VMEM((dw.shape[0], SUBLANES, d), _F32),
         pltpu.VMEM((SEQ_TILE, d), _BF16)],
        casts)


def _row(v):
    return v.reshape(1, -1)


def kernel(x, ln1_0, a0_w_in, a0_conv, a0_w_out, ln2_0, ffn0_w_gu, ffn0_w_down, ln1_1, b1_w_grp, b1_scale, ln2_1, ffn1_w_gu, ffn1_w_down, ln1_2, c2_w_pw1, c2_b_pw1, c2_dw, c2_b_dw, c2_ln_g, c2_ln_b, c2_w_pw2, c2_b_pw2, ln2_2, ffn2_w_gu, ffn2_w_down, ln1_3, a3_w_in, a3_conv, a3_w_out, ln2_3, ffn3_w_gu, ffn3_w_down, ln_f):
    h, (wgu0, wdown0) = _short_conv(
        x, _row(ln1_0), a0_w_in.astype(_BF16), a0_conv, a0_w_out.astype(_BF16),
        casts=(ffn0_w_gu, ffn0_w_down))
    h, (wgrp, wgu1, wdown1) = _ffn(
        h, _row(ln2_0), wgu0, wdown0, casts=(b1_w_grp, ffn1_w_gu, ffn1_w_down))
    h, (w_pw1, w_pw2) = _pool_ffn(
        h, _row(ln1_1), wgrp, _row(b1_scale), _row(ln2_1), wgu1, wdown1,
        casts=(c2_w_pw1, c2_w_pw2))
    h, (wgu2, wdown2) = _conformer(
        h, _row(ln1_2), w_pw1, _row(c2_b_pw1), c2_dw, _row(c2_b_dw), _row(c2_ln_g),
        _row(c2_ln_b), w_pw2, _row(c2_b_pw2), casts=(ffn2_w_gu, ffn2_w_down))
    h, (w_in3, w_out3) = _ffn(h, _row(ln2_2), wgu2, wdown2, casts=(a3_w_in, a3_w_out))
    h, (wgu3, wdown3) = _short_conv(
        h, _row(ln1_3), w_in3, a3_conv, w_out3, casts=(ffn3_w_gu, ffn3_w_down))
    return _ffn_final(h, _row(ln2_3), wgu3, wdown3, _row(ln_f))
```

```python
import math

import jax
import jax.numpy as jnp
from jax import lax
from jax.experimental import pallas as pl
from jax.experimental.pallas import tpu as pltpu

RMS_EPS = 1e-6
LN_EPS = 1e-5
POOL_WINDOWS = (2, 4, 8, 16)

SEQ_TILE = 512
FFN_TILE = 1024
CONV_TILE = 1024
FFN_CHUNK = 256
MXU_COLS = 256
SUBLANES = 8
BF16_ROWS = 16
VMEM_LIMIT = 58 * 1024 * 1024

_BF16 = jnp.bfloat16
_F32 = jnp.float32


def _dot(a, b):
    return jnp.dot(a, b, preferred_element_type=_F32)


def _rmsnorm(x, g):
    ms = jnp.mean(x * x, axis=-1, keepdims=True)
    return x * lax.rsqrt(ms + RMS_EPS) * g


def _resident(shape):
    zeros = (0,) * len(shape)
    return pl.BlockSpec(shape, lambda *_: zeros, pipeline_mode=pl.Buffered(1))


def _cast_blocks(shape, steps):
    if len(shape) > 2:
        return math.gcd(shape[0], steps)
    nb = steps
    while shape[0] % (nb * BF16_ROWS):
        nb //= 2
    return nb


def _tile_call(body, name, h, residents, scratch_shapes, casts=(), tile=SEQ_TILE):
    bn, s, d = h.shape
    tiles = s // tile
    steps = bn * tiles
    n_in = 1 + len(residents)
    n_cast = len(casts)
    row_tile = pl.BlockSpec((None, tile, d), lambda b, t: (b, t, 0))

    cast_specs = []
    for w in casts:
        nb = _cast_blocks(w.shape, steps)
        tail = (0,) * (w.ndim - 1)
        cast_specs.append(pl.BlockSpec(
            (w.shape[0] // nb,) + w.shape[1:],
            lambda b, t, nb=nb, tail=tail: (((b * tiles + t) * nb) // steps,) + tail))

    def wrapped(*refs):
        srcs = refs[n_in:n_in + n_cast]
        dsts = refs[n_in + n_cast + 1:n_in + 2 * n_cast + 1]
        for src, dst in zip(srcs, dsts):
            dst[...] = src[...].astype(_BF16)
        body(*refs[:n_in], refs[n_in + n_cast], *refs[n_in + 2 * n_cast + 1:])

    outs = pl.pallas_call(
        wrapped,
        grid=(bn, tiles),
        in_specs=[row_tile] + [_resident(r.shape) for r in residents] + cast_specs,
        out_specs=[row_tile] + cast_specs,
        out_shape=[jax.ShapeDtypeStruct(h.shape, h.dtype)]
        + [jax.ShapeDtypeStruct(w.shape, _BF16) for w in casts],
        scratch_shapes=scratch_shapes,
        compiler_params=pltpu.CompilerParams(
            dimension_semantics=("arbitrary", "arbitrary"),
            vmem_limit_bytes=VMEM_LIMIT),
        name=name,
    )(h, *residents, *casts)
    return outs[0], outs[1:]


def _alternate(a, n_a, b, n_b):
    i_a = i_b = 0
    while i_a < n_a or i_b < n_b:
        if i_b >= n_b or (i_a < n_a and i_a * n_b <= i_b * n_a):
            next(a)
            i_a += 1
        else:
            next(b)
            i_b += 1


def _layer_call(prepare, stage, pieces, head_pieces, name, h, residents, scratch_shapes, ffn,
                casts=()):
    bn, s, d = h.shape
    per_seq = s // SEQ_TILE
    tiles = bn * per_seq
    n_mix = 1 + len(residents)
    n_in = n_mix + len(ffn)
    n_cast = len(casts)

    def tile_spec(tile_of_step):
        def index(t):
            tile = tile_of_step(t)
            return tile // per_seq, tile % per_seq, 0
        return pl.BlockSpec((None, SEQ_TILE, d), index)

    mixer_tile = lambda t: jnp.minimum(t, tiles - 1)
    cast_specs = []
    for w in casts:
        nb = _cast_blocks(w.shape, tiles)
        tail = (0,) * (w.ndim - 1)
        cast_specs.append(pl.BlockSpec(
            (w.shape[0] // nb,) + w.shape[1:],
            lambda t, nb=nb, tail=tail: ((mixer_tile(t) * nb) // tiles,) + tail))

    def wrapped(*refs):
        srcs = refs[n_in:n_in + n_cast]
        o_ref = refs[n_in + n_cast]
        dsts = refs[n_in + n_cast + 1:n_in + 2 * n_cast + 1]
        mid_ref, act_ref = refs[n_in + 2 * n_cast + 1:n_in + 2 * n_cast + 3]
        t = pl.program_id(0)
        for src, dst in zip(srcs, dsts):
            dst[...] = src[...].astype(_BF16)

        @pl.when(t == 0)
        def _():
            mid_ref[...] = jnp.zeros(mid_ref.shape, _F32)

        stage_args = refs[:n_mix] + (mid_ref,) + refs[n_in + 2 * n_cast + 3:]
        seq_tile = mixer_tile(t) % per_seq
        prepare(*stage_args, first_step=t == 0, seq_tile=seq_tile)
        mixer = stage(*stage_args, seq_tile=seq_tile)
        for _ in range(head_pieces):
            next(mixer)
        swiglu = _swiglu_pieces(mid_ref, *refs[n_mix:n_in], act_ref, o_ref)
        _alternate(swiglu, _swiglu_piece_count(ffn[2].shape), mixer, pieces - head_pieces)

    w_down = ffn[2]
    outs = pl.pallas_call(
        wrapped,
        grid=(tiles + 1,),
        in_specs=[tile_spec(mixer_tile)] + [_resident(r.shape) for r in residents + list(ffn)]
        + cast_specs,
        out_specs=[tile_spec(lambda t: jnp.maximum(t - 1, 0))] + cast_specs,
        out_shape=[jax.ShapeDtypeStruct(h.shape, h.dtype)]
        + [jax.ShapeDtypeStruct(w.shape, _BF16) for w in casts],
        scratch_shapes=[pltpu.VMEM((SEQ_TILE, d), _F32), _act_scratch(w_down)] + scratch_shapes,
        compiler_params=pltpu.CompilerParams(
            dimension_semantics=("arbitrary",),
            vmem_limit_bytes=VMEM_LIMIT),
        name=name,
    )(h, *residents, *ffn, *casts)
    return outs[0], outs[1:]


def _swiglu_piece_count(w_down_shape):
    return 1 + w_down_shape[0] // FFN_CHUNK + w_down_shape[1] // MXU_COLS


def _swiglu_pieces(x_ref, g_ref, wgu_ref, wdown_ref, act_ref, o_ref, lnf_ref=None):
    f, d = wdown_ref.shape
    x = x_ref[...]
    o_ref[...] = x
    xn = _rmsnorm(x, g_ref[...]).astype(_BF16)
    yield
    for j in range(f // FFN_CHUNK):
        lo = j * FFN_CHUNK
        gate = _dot(xn, wgu_ref[:, lo:lo + FFN_CHUNK])
        up = _dot(xn, wgu_ref[:, f + lo:f + lo + FFN_CHUNK])
        act_ref[:, lo:lo + FFN_CHUNK] = (gate * jax.nn.sigmoid(gate) * up).astype(_BF16)
        yield
    for c0 in range(0, d, MXU_COLS):
        o_ref[:, c0:c0 + MXU_COLS] += _dot(act_ref[...], wdown_ref[:, c0:c0 + MXU_COLS])
        yield
    if lnf_ref is not None:
        o_ref[...] = _rmsnorm(o_ref[...], lnf_ref[...])
        yield


def _staggered(gens, lag):
    done = [False] * len(gens)
    turn = 0
    while not all(done):
        for i, gen in enumerate(gens):
            if turn >= i * lag and not done[i]:
                done[i] = next(gen, True) is True
        turn += 1


def _ffn_kernel(h_ref, g_ref, wgu_ref, wdown_ref, *rest):
    *lnf, o_ref, act_ref = rest
    half = h_ref.shape[0] // 2
    rows = [pl.ds(i * half, half) for i in range(2)]
    _staggered([_swiglu_pieces(h_ref.at[r], g_ref, wgu_ref, wdown_ref, act_ref.at[r],
                               o_ref.at[r], *lnf) for r in rows], lag=1)


def _act_scratch(w_down, tile=SEQ_TILE):
    return pltpu.VMEM((tile, w_down.shape[0]), _BF16)


def _ffn(h, g, w_gu, w_down, casts):
    return _tile_call(_ffn_kernel, "ffn", h, [g, w_gu, w_down],
                      [_act_scratch(w_down, FFN_TILE)], casts, tile=FFN_TILE)


def _ffn_final(h, g, w_gu, w_down, ln_f):
    return _tile_call(_ffn_kernel, "ffn_final", h, [g, w_gu, w_down, ln_f],
                      [_act_scratch(w_down, FFN_TILE)], tile=FFN_TILE)[0]


def _short_conv_pieces(rows, h_ref, g_ref, win_ref, cw_ref, wout_ref, o_ref, u_ref, y_ref):
    d = h_ref.shape[-1]
    k = cw_ref.shape[0]
    first = SUBLANES + rows.start
    xn = _rmsnorm(h_ref[rows, :], g_ref[...]).astype(_BF16)
    yield
    gates = []
    for c0 in range(0, d, MXU_COLS):
        cols = slice(c0, c0 + MXU_COLS)
        b, c, v = (_dot(xn, win_ref[:, i * d + c0:i * d + c0 + MXU_COLS]) for i in range(3))
        gates.append(b)
        u_ref[first:first + rows.size, cols] = c * v
        yield
    for gate_b, c0 in zip(gates, range(0, d, MXU_COLS)):
        cols = slice(c0, c0 + MXU_COLS)
        conv = cw_ref[k - 1:k, cols] * u_ref[first:first + rows.size, cols]
        for j in range(1, k):
            conv = conv + cw_ref[k - 1 - j:k - j, cols] * u_ref[first - j:first - j + rows.size, cols]
        y_ref[rows, cols] = (gate_b * conv).astype(_BF16)
        yield
    for c0 in range(0, d, MXU_COLS):
        cols = slice(c0, c0 + MXU_COLS)
        o_ref[rows, cols] = h_ref[rows, cols] + _dot(y_ref[rows, :], wout_ref[:, cols])
        yield


def _short_conv_kernel(h_ref, g_ref, win_ref, cw_ref, wout_ref, o_ref, u_ref, y_ref):
    d = h_ref.shape[-1]
    tile = h_ref.shape[0]
    hist = SUBLANES

    @pl.when(pl.program_id(1) == 0)
    def _():
        u_ref[0:hist, :] = jnp.zeros((hist, d), _F32)

    halves = [pl.ds(i * (tile // 2), tile // 2) for i in range(2)]
    _staggered([_short_conv_pieces(rows, h_ref, g_ref, win_ref, cw_ref, wout_ref, o_ref,
                                   u_ref, y_ref) for rows in halves], lag=d // MXU_COLS)
    u_ref[0:hist, :] = u_ref[tile:tile + hist, :]


def _short_conv(h, g, w_in, conv_w, w_out, casts):
    d = h.shape[-1]
    return _tile_call(_short_conv_kernel, "short_conv", h, [g, w_in, conv_w, w_out],
                      [pltpu.VMEM((SUBLANES + CONV_TILE, d), _F32),
                       pltpu.VMEM((CONV_TILE, d), _BF16)], casts, tile=CONV_TILE)


def _pool_prepare(h_ref, g_ref, wgrp_ref, scale_ref, dst_ref, xn_ref, *, first_step, seq_tile):
    @pl.when(seq_tile == 0)
    def _():
        xn_ref[0:max(POOL_WINDOWS), :] = jnp.zeros((max(POOL_WINDOWS), xn_ref.shape[1]), _F32)


def _pool_stage(h_ref, g_ref, wgrp_ref, scale_ref, dst_ref, xn_ref, *, seq_tile):
    d = h_ref.shape[-1]
    hist = max(POOL_WINDOWS)
    gw = d // len(POOL_WINDOWS)
    xn_ref[hist:hist + SEQ_TILE, :] = _rmsnorm(h_ref[...], g_ref[...])
    yield
    t = seq_tile * SEQ_TILE + lax.broadcasted_iota(jnp.int32, (SEQ_TILE, 1), 0)
    for gi, w in enumerate(POOL_WINDOWS):
        c0 = gi * gw
        cur = xn_ref[hist:hist + SEQ_TILE, c0:c0 + gw]
        tot = cur
        for j in range(1, w):
            tot = tot + xn_ref[hist - j:hist - j + SEQ_TILE, c0:c0 + gw]
        cnt = jnp.minimum(t + 1, w).astype(_F32)
        mixed = (tot / cnt - cur).astype(_BF16)
        y = _dot(mixed, wgrp_ref[gi]) * scale_ref[:, c0:c0 + gw]
        dst_ref[:, c0:c0 + gw] = h_ref[:, c0:c0 + gw] + y
        if gi == len(POOL_WINDOWS) - 1:
            xn_ref[0:hist, :] = xn_ref[SEQ_TILE:SEQ_TILE + hist, :]
        yield


def _pool_ffn(h, g1, w_grp, scale, g2, w_gu, w_down, casts):
    d = h.shape[-1]
    return _layer_call(_pool_prepare, _pool_stage, 1 + len(POOL_WINDOWS), 0, "pool_ffn", h,
                       [g1, w_grp, scale],
                       [pltpu.VMEM((max(POOL_WINDOWS) + SEQ_TILE, d), _F32)],
                       (g2, w_gu, w_down), casts)


CONF_ROWS = 32
CONV_BLOCK = 128


def _conformer_prepare(h_ref, g_ref, w1_ref, b1_ref, dw_ref, bdw_ref, lng_ref, lnb_ref,
                       w2_ref, b2_ref, dst_ref, glu_ref, shift_ref, dwb_ref, z_ref,
                       *, first_step, seq_tile):
    d = h_ref.shape[-1]
    hist = glu_ref.shape[0] - SEQ_TILE

    @pl.when(first_step)
    def _():
        for j in range(dw_ref.shape[0]):
            dwb_ref[j] = jnp.broadcast_to(dw_ref[j:j + 1, :], (SUBLANES, d))

    @pl.when(seq_tile == 0)
    def _():
        glu_ref[0:hist, :] = jnp.zeros((hist, d), _F32)


def _conformer_stage(h_ref, g_ref, w1_ref, b1_ref, dw_ref, bdw_ref, lng_ref, lnb_ref,
                     w2_ref, b2_ref, dst_ref, glu_ref, shift_ref, dwb_ref, z_ref, *, seq_tile):
    del seq_tile
    d = h_ref.shape[-1]
    k = dw_ref.shape[0]
    hist = glu_ref.shape[0] - SEQ_TILE
    xn = _rmsnorm(h_ref[...], g_ref[...]).astype(_BF16)
    yield
    for c0 in range(0, d, MXU_COLS):
        a = _dot(xn, w1_ref[:, c0:c0 + MXU_COLS]) + b1_ref[:, c0:c0 + MXU_COLS]
        gate = _dot(xn, w1_ref[:, d + c0:d + c0 + MXU_COLS]) + b1_ref[:, d + c0:d + c0 + MXU_COLS]
        glu_ref[hist:hist + SEQ_TILE, c0:c0 + MXU_COLS] = a * jax.nn.sigmoid(gate)
        yield

    groups = CONF_ROWS // SUBLANES
    n = CONV_BLOCK + hist
    for b0 in range(0, SEQ_TILE, CONV_BLOCK):
        for r in range(1, SUBLANES):
            rolled = pltpu.roll(glu_ref[b0:b0 + n, :], n - r, axis=0)
            shift_ref[r - 1] = rolled[0:n - SUBLANES, :]
        yield
        for r0 in range(0, CONV_BLOCK, CONF_ROWS):
            rows = slice(b0 + r0, b0 + r0 + CONF_ROWS)
            acc = jnp.zeros((groups, SUBLANES, d), _F32) + bdw_ref[...]
            for j in range(k):
                off = hist - (k - 1) + j
                r = off % SUBLANES
                base = r0 + off - r
                if r == 0:
                    win = glu_ref[b0 + base:b0 + base + CONF_ROWS, :]
                else:
                    win = shift_ref[r - 1, base:base + CONF_ROWS, :]
                acc = acc + dwb_ref[j] * win.reshape(groups, SUBLANES, d)
            acc = acc.reshape(CONF_ROWS, d)
            mu = jnp.mean(acc, axis=-1, keepdims=True)
            xc = acc - mu
            var = jnp.mean(xc * xc, axis=-1, keepdims=True)
            ln = xc * lax.rsqrt(var + LN_EPS) * lng_ref[...] + lnb_ref[...]
            z_ref[rows, :] = (ln * jax.nn.sigmoid(ln)).astype(_BF16)
            yield
    glu_ref[0:hist, :] = glu_ref[SEQ_TILE:SEQ_TILE + hist, :]
    for c0 in range(0, d, MXU_COLS):
        y = _dot(z_ref[...], w2_ref[:, c0:c0 + MXU_COLS]) + b2_ref[:, c0:c0 + MXU_COLS]
        dst_ref[:, c0:c0 + MXU_COLS] = h_ref[:, c0:c0 + MXU_COLS] + y
        yield


def _conformer_kernel(*refs):
    first_step = (pl.program_id(0) == 0) & (pl.program_id(1) == 0)
    _conformer_prepare(*refs, first_step=first_step, seq_tile=pl.program_id(1))
    for _ in _conformer_stage(*refs, seq_tile=pl.program_id(1)):
        pass


def _conformer(h, g, w1, b1, dw, bdw, lng, lnb, w2, b2, casts):
    d = h.shape[-1]
    hist = -(-(dw.shape[0] - 1) // SUBLANES) * SUBLANES
    return _tile_call(
        _conformer_kernel, "conformer", h, [g, w1, b1, dw, bdw, lng, lnb, w2, b2],
        [pltpu.VMEM((hist + SEQ_TILE, d), _F32),
         pltpu.VMEM((SUBLANES - 1, hist + CONV_BLOCK - SUBLANES, d), _F32),
         pltpu.VMEM((dw.shape[0], SUBLANES, d), _F32),
         pltpu.VMEM((SEQ_TILE, d), _BF16)],
        casts)


def _row(v):
    return v.reshape(1, -1)


def kernel(x, ln1_0, a0_w_in, a0_conv, a0_w_out, ln2_0, ffn0_w_gu, ffn0_w_down, ln1_1, b1_w_grp, b1_scale, ln2_1, ffn1_w_gu, ffn1_w_down, ln1_2, c2_w_pw1, c2_b_pw1, c2_dw, c2_b_dw, c2_ln_g, c2_ln_b, c2_w_pw2, c2_b_pw2, ln2_2, ffn2_w_gu, ffn2_w_down, ln1_3, a3_w_in, a3_conv, a3_w_out, ln2_3, ffn3_w_gu, ffn3_w_down, ln_f):
    h, (wgu0, wdown0) = _short_conv(
        x, _row(ln1_0), a0_w_in.astype(_BF16), a0_conv, a0_w_out.astype(_BF16),
        casts=(ffn0_w_gu, ffn0_w_down))
    h, (wgrp, wgu1, wdown1) = _ffn(
        h, _row(ln2_0), wgu0, wdown0, casts=(b1_w_grp, ffn1_w_gu, ffn1_w_down))
    h, (w_pw1, w_pw2) = _pool_ffn(
        h, _row(ln1_1), wgrp, _row(b1_scale), _row(ln2_1), wgu1, wdown1,
        casts=(c2_w_pw1, c2_w_pw2))
    h, (wgu2, wdown2) = _conformer(
        h, _row(ln1_2), w_pw1, _row(c2_b_pw1), c2_dw, _row(c2_b_dw), _row(c2_ln_g),
        _row(c2_ln_b), w_pw2, _row(c2_b_pw2), casts=(ffn2_w_gu, ffn2_w_down))
    h, (w_in3, w_out3) = _ffn(h, _row(ln2_2), wgu2, wdown2, casts=(a3_w_in, a3_w_out))
    h, (wgu3, wdown3) = _short_conv(
        h, _row(ln1_3), w_in3, a3_conv, w_out3, casts=(ffn3_w_gu, ffn3_w_down))
    return _ffn_final(h, _row(ln2_3), wgu3, wdown3, _row(ln_f))
```

```python
import math

import jax
import jax.numpy as jnp
from jax import lax
from jax.experimental import pallas as pl
from jax.experimental.pallas import tpu as pltpu

RMS_EPS = 1e-6
LN_EPS = 1e-5
POOL_WINDOWS = (2, 4, 8, 16)

SEQ_TILE = 512
FFN_TILE = 1024
CONV_TILE = 1024
FFN_CHUNK = 256
MXU_COLS = 256
SUBLANES = 8
BF16_ROWS = 16
VMEM_LIMIT = 58 * 1024 * 1024

_BF16 = jnp.bfloat16
_F32 = jnp.float32


def _dot(a, b):
    return jnp.dot(a, b, preferred_element_type=_F32)


def _rmsnorm(x, g):
    ms = jnp.mean(x * x, axis=-1, keepdims=True)
    return x * lax.rsqrt(ms + RMS_EPS) * g


def _resident(shape):
    zeros = (0,) * len(shape)
    return pl.BlockSpec(shape, lambda *_: zeros, pipeline_mode=pl.Buffered(1))


def _cast_blocks(shape, steps):
    if len(shape) > 2:
        return math.gcd(shape[0], steps)
    nb = steps
    while shape[0] % (nb * BF16_ROWS):
        nb //= 2
    return nb


def _tile_call(body, name, h, residents, scratch_shapes, casts=(), tile=SEQ_TILE):
    bn, s, d = h.shape
    tiles = s // tile
    steps = bn * tiles
    n_in = 1 + len(residents)
    n_cast = len(casts)
    row_tile = pl.BlockSpec((None, tile, d), lambda b, t: (b, t, 0))

    cast_specs = []
    for w in casts:
        nb = _cast_blocks(w.shape, steps)
        tail = (0,) * (w.ndim - 1)
        cast_specs.append(pl.BlockSpec(
            (w.shape[0] // nb,) + w.shape[1:],
            lambda b, t, nb=nb, tail=tail: (((b * tiles + t) * nb) // steps,) + tail))

    def wrapped(*refs):
        srcs = refs[n_in:n_in + n_cast]
        dsts = refs[n_in + n_cast + 1:n_in + 2 * n_cast + 1]
        for src, dst in zip(srcs, dsts):
            dst[...] = src[...].astype(_BF16)
        body(*refs[:n_in], refs[n_in + n_cast], *refs[n_in + 2 * n_cast + 1:])

    outs = pl.pallas_call(
        wrapped,
        grid=(bn, tiles),
        in_specs=[row_tile] + [_resident(r.shape) for r in residents] + cast_specs,
        out_specs=[row_tile] + cast_specs,
        out_shape=[jax.ShapeDtypeStruct(h.shape, h.dtype)]
        + [jax.ShapeDtypeStruct(w.shape, _BF16) for w in casts],
        scratch_shapes=scratch_shapes,
        compiler_params=pltpu.CompilerParams(
            dimension_semantics=("arbitrary", "arbitrary"),
            vmem_limit_bytes=VMEM_LIMIT),
        name=name,
    )(h, *residents, *casts)
    return outs[0], outs[1:]


def _alternate(a, n_a, b, n_b):
    i_a = i_b = 0
    while i_a < n_a or i_b < n_b:
        if i_b >= n_b or (i_a < n_a and i_a * n_b <= i_b * n_a):
            next(a)
            i_a += 1
        else:
            next(b)
            i_b += 1


def _layer_call(prepare, stage, pieces, head_pieces, name, h, residents, scratch_shapes, ffn,
                casts=()):
    bn, s, d = h.shape
    per_seq = s // SEQ_TILE
    tiles = bn * per_seq
    n_mix = 1 + len(residents)
    n_in = n_mix + len(ffn)
    n_cast = len(casts)

    def tile_spec(tile_of_step):
        def index(t):
            tile = tile_of_step(t)
            return tile // per_seq, tile % per_seq, 0
        return pl.BlockSpec((None, SEQ_TILE, d), index)

    mixer_tile = lambda t: jnp.minimum(t, tiles - 1)
    cast_specs = []
    for w in casts:
        nb = _cast_blocks(w.shape, tiles)
        tail = (0,) * (w.ndim - 1)
        cast_specs.append(pl.BlockSpec(
            (w.shape[0] // nb,) + w.shape[1:],
            lambda t, nb=nb, tail=tail: ((mixer_tile(t) * nb) // tiles,) + tail))

    def wrapped(*refs):
        srcs = refs[n_in:n_in + n_cast]
        o_ref = refs[n_in + n_cast]
        dsts = refs[n_in + n_cast + 1:n_in + 2 * n_cast + 1]
        mid_ref, act_ref = refs[n_in + 2 * n_cast + 1:n_in + 2 * n_cast + 3]
        t = pl.program_id(0)
        for src, dst in zip(srcs, dsts):
            dst[...] = src[...].astype(_BF16)

        @pl.when(t == 0)
        def _():
            mid_ref[...] = jnp.zeros(mid_ref.shape, _F32)

        stage_args = refs[:n_mix] + (mid_ref,) + refs[n_in + 2 * n_cast + 3:]
        seq_tile = mixer_tile(t) % per_seq
        prepare(*stage_args, first_step=t == 0, seq_tile=seq_tile)
        mixer = stage(*stage_args, seq_tile=seq_tile)
        for _ in range(head_pieces):
            next(mixer)
        swiglu = _swiglu_pieces(mid_ref, *refs[n_mix:n_in], act_ref, o_ref)
        _alternate(swiglu, _swiglu_piece_count(ffn[2].shape), mixer, pieces - head_pieces)

    w_down = ffn[2]
    outs = pl.pallas_call(
        wrapped,
        grid=(tiles + 1,),
        in_specs=[tile_spec(mixer_tile)] + [_resident(r.shape) for r in residents + list(ffn)]
        + cast_specs,
        out_specs=[tile_spec(lambda t: jnp.maximum(t - 1, 0))] + cast_specs,
        out_shape=[jax.ShapeDtypeStruct(h.shape, h.dtype)]
        + [jax.ShapeDtypeStruct(w.shape, _BF16) for w in casts],
        scratch_shapes=[pltpu.VMEM((SEQ_TILE, d), _F32), _act_scratch(w_down)] + scratch_shapes,
        compiler_params=pltpu.CompilerParams(
            dimension_semantics=("arbitrary",),
            vmem_limit_bytes=VMEM_LIMIT),
        name=name,
    )(h, *residents, *ffn, *casts)
    return outs[0], outs[1:]


def _swiglu_piece_count(w_down_shape):
    return 1 + w_down_shape[0] // FFN_CHUNK + w_down_shape[1] // MXU_COLS


def _swiglu_pieces(x_ref, g_ref, wgu_ref, wdown_ref, act_ref, o_ref, lnf_ref=None):
    f, d = wdown_ref.shape
    x = x_ref[...]
    o_ref[...] = x
    xn = _rmsnorm(x, g_ref[...]).astype(_BF16)
    yield
    for j in range(f // FFN_CHUNK):
        lo = j * FFN_CHUNK
        gate = _dot(xn, wgu_ref[:, lo:lo + FFN_CHUNK])
        up = _dot(xn, wgu_ref[:, f + lo:f + lo + FFN_CHUNK])
        act_ref[:, lo:lo + FFN_CHUNK] = (gate * jax.nn.sigmoid(gate) * up).astype(_BF16)
        yield
    for c0 in range(0, d, MXU_COLS):
        o_ref[:, c0:c0 + MXU_COLS] += _dot(act_ref[...], wdown_ref[:, c0:c0 + MXU_COLS])
        yield
    if lnf_ref is not None:
        o_ref[...] = _rmsnorm(o_ref[...], lnf_ref[...])
        yield


def _staggered(gens, lag):
    done = [False] * len(gens)
    turn = 0
    while not all(done):
        for i, gen in enumerate(gens):
            if turn >= i * lag and not done[i]:
                done[i] = next(gen, True) is True
        turn += 1


def _ffn_kernel(h_ref, g_ref, wgu_ref, wdown_ref, *rest):
    *lnf, o_ref, act_ref = rest
    half = h_ref.shape[0] // 2
    rows = [pl.ds(i * half, half) for i in range(2)]
    _staggered([_swiglu_pieces(h_ref.at[r], g_ref, wgu_ref, wdown_ref, act_ref.at[r],
                               o_ref.at[r], *lnf) for r in rows], lag=1)


def _act_scratch(w_down, tile=SEQ_TILE):
    return pltpu.VMEM((tile, w_down.shape[0]), _BF16)


def _ffn(h, g, w_gu, w_down, casts):
    return _tile_call(_ffn_kernel, "ffn", h, [g, w_gu, w_down],
                      [_act_scratch(w_down, FFN_TILE)], casts, tile=FFN_TILE)


def _ffn_final(h, g, w_gu, w_down, ln_f):
    return _tile_call(_ffn_kernel, "ffn_final", h, [g, w_gu, w_down, ln_f],
                      [_act_scratch(w_down, FFN_TILE)], tile=FFN_TILE)[0]


def _short_conv_pieces(rows, h_ref, g_ref, win_ref, cw_ref, wout_ref, o_ref, u_ref, y_ref):
    d = h_ref.shape[-1]
    k = cw_ref.shape[0]
    first = SUBLANES + rows.start
    xn = _rmsnorm(h_ref[rows, :], g_ref[...]).astype(_BF16)
    yield
    gates = []
    for c0 in range(0, d, MXU_COLS):
        cols = slice(c0, c0 + MXU_COLS)
        b, c, v = (_dot(xn, win_ref[:, i * d + c0:i * d + c0 + MXU_COLS]) for i in range(3))
        gates.append(b)
        u_ref[first:first + rows.size, cols] = c * v
        yield
    for gate_b, c0 in zip(gates, range(0, d, MXU_COLS)):
        cols = slice(c0, c0 + MXU_COLS)
        conv = cw_ref[k - 1:k, cols] * u_ref[first:first + rows.size, cols]
        for j in range(1, k):
            conv = conv + cw_ref[k - 1 - j:k - j, cols] * u_ref[first - j:first - j + rows.size, cols]
        y_ref[rows, cols] = (gate_b * conv).astype(_BF16)
        yield
    for c0 in range(0, d, MXU_COLS):
        cols = slice(c0, c0 + MXU_COLS)
        o_ref[rows, cols] = h_ref[rows, cols] + _dot(y_ref[rows, :], wout_ref[:, cols])
        yield


def _short_conv_kernel(h_ref, g_ref, win_ref, cw_ref, wout_ref, o_ref, u_ref, y_ref, *own_bf16):
    d = h_ref.shape[-1]
    tile = h_ref.shape[0]
    hist = SUBLANES

    if own_bf16:
        @pl.when((pl.program_id(0) == 0) & (pl.program_id(1) == 0))
        def _():
            for src, dst in zip((win_ref, wout_ref), own_bf16):
                dst[...] = src[...].astype(_BF16)
        win_ref, wout_ref = own_bf16

    @pl.when(pl.program_id(1) == 0)
    def _():
        u_ref[0:hist, :] = jnp.zeros((hist, d), _F32)

    halves = [pl.ds(i * (tile // 2), tile // 2) for i in range(2)]
    _staggered([_short_conv_pieces(rows, h_ref, g_ref, win_ref, cw_ref, wout_ref, o_ref,
                                   u_ref, y_ref) for rows in halves], lag=d // MXU_COLS)
    u_ref[0:hist, :] = u_ref[tile:tile + hist, :]


def _short_conv(h, g, w_in, conv_w, w_out, casts):
    d = h.shape[-1]
    own_bf16 = [pltpu.VMEM(w.shape, _BF16) for w in (w_in, w_out) if w.dtype != _BF16]
    return _tile_call(_short_conv_kernel, "short_conv", h, [g, w_in, conv_w, w_out],
                      [pltpu.VMEM((SUBLANES + CONV_TILE, d), _F32),
                       pltpu.VMEM((CONV_TILE, d), _BF16)] + own_bf16, casts, tile=CONV_TILE)


def _pool_prepare(h_ref, g_ref, wgrp_ref, scale_ref, dst_ref, xn_ref, *, first_step, seq_tile):
    @pl.when(seq_tile == 0)
    def _():
        xn_ref[0:max(POOL_WINDOWS), :] = jnp.zeros((max(POOL_WINDOWS), xn_ref.shape[1]), _F32)


def _pool_stage(h_ref, g_ref, wgrp_ref, scale_ref, dst_ref, xn_ref, *, seq_tile):
    d = h_ref.shape[-1]
    hist = max(POOL_WINDOWS)
    gw = d // len(POOL_WINDOWS)
    xn_ref[hist:hist + SEQ_TILE, :] = _rmsnorm(h_ref[...], g_ref[...])
    yield
    t = seq_tile * SEQ_TILE + lax.broadcasted_iota(jnp.int32, (SEQ_TILE, 1), 0)
    for gi, w in enumerate(POOL_WINDOWS):
        c0 = gi * gw
        cur = xn_ref[hist:hist + SEQ_TILE, c0:c0 + gw]
        tot = cur
        for j in range(1, w):
            tot = tot + xn_ref[hist - j:hist - j + SEQ_TILE, c0:c0 + gw]
        cnt = jnp.minimum(t + 1, w).astype(_F32)
        mixed = (tot / cnt - cur).astype(_BF16)
        y = _dot(mixed, wgrp_ref[gi]) * scale_ref[:, c0:c0 + gw]
        dst_ref[:, c0:c0 + gw] = h_ref[:, c0:c0 + gw] + y
        if gi == len(POOL_WINDOWS) - 1:
            xn_ref[0:hist, :] = xn_ref[SEQ_TILE:SEQ_TILE + hist, :]
        yield


def _pool_ffn(h, g1, w_grp, scale, g2, w_gu, w_down, casts):
    d = h.shape[-1]
    return _layer_call(_pool_prepare, _pool_stage, 1 + len(POOL_WINDOWS), 0, "pool_ffn", h,
                       [g1, w_grp, scale],
                       [pltpu.VMEM((max(POOL_WINDOWS) + SEQ_TILE, d), _F32)],
                       (g2, w_gu, w_down), casts)


CONF_ROWS = 32
CONV_BLOCK = 256


def _conformer_prepare(h_ref, g_ref, w1_ref, b1_ref, dw_ref, bdw_ref, lng_ref, lnb_ref,
                       w2_ref, b2_ref, dst_ref, glu_ref, shift_ref, dwb_ref, z_ref,
                       *, first_step, seq_tile):
    d = h_ref.shape[-1]
    hist = glu_ref.shape[0] - SEQ_TILE

    @pl.when(first_step)
    def _():
        for j in range(dw_ref.shape[0]):
            dwb_ref[j] = jnp.broadcast_to(dw_ref[j:j + 1, :], (SUBLANES, d))

    @pl.when(seq_tile == 0)
    def _():
        glu_ref[0:hist, :] = jnp.zeros((hist, d), _F32)


def _conformer_stage(h_ref, g_ref, w1_ref, b1_ref, dw_ref, bdw_ref, lng_ref, lnb_ref,
                     w2_ref, b2_ref, dst_ref, glu_ref, shift_ref, dwb_ref, z_ref, *, seq_tile):
    del seq_tile
    d = h_ref.shape[-1]
    k = dw_ref.shape[0]
    hist = glu_ref.shape[0] - SEQ_TILE
    xn = _rmsnorm(h_ref[...], g_ref[...]).astype(_BF16)
    yield
    for c0 in range(0, d, MXU_COLS):
        a = _dot(xn, w1_ref[:, c0:c0 + MXU_COLS]) + b1_ref[:, c0:c0 + MXU_COLS]
        gate = _dot(xn, w1_ref[:, d + c0:d + c0 + MXU_COLS]) + b1_ref[:, d + c0:d + c0 + MXU_COLS]
        glu_ref[hist:hist + SEQ_TILE, c0:c0 + MXU_COLS] = a * jax.nn.sigmoid(gate)
        yield

    groups = CONF_ROWS // SUBLANES
    n = CONV_BLOCK + hist
    for b0 in range(0, SEQ_TILE, CONV_BLOCK):
        for r in range(1, SUBLANES):
            rolled = pltpu.roll(glu_ref[b0:b0 + n, :], n - r, axis=0)
            shift_ref[r - 1] = rolled[0:n - SUBLANES, :]
        yield
        for r0 in range(0, CONV_BLOCK, CONF_ROWS):
            rows = slice(b0 + r0, b0 + r0 + CONF_ROWS)
            acc = jnp.zeros((groups, SUBLANES, d), _F32) + bdw_ref[...]
            for j in range(k):
                off = hist - (k - 1) + j
                r = off % SUBLANES
                base = r0 + off - r
                if r == 0:
                    win = glu_ref[b0 + base:b0 + base + CONF_ROWS, :]
                else:
                    win = shift_ref[r - 1, base:base + CONF_ROWS, :]
                acc = acc + dwb_ref[j] * win.reshape(groups, SUBLANES, d)
            acc = acc.reshape(CONF_ROWS, d)
            mu = jnp.mean(acc, axis=-1, keepdims=True)
            xc = acc - mu
            var = jnp.mean(xc * xc, axis=-1, keepdims=True)
            ln = xc * lax.rsqrt(var + LN_EPS) * lng_ref[...] + lnb_ref[...]
            z_ref[rows, :] = (ln * jax.nn.sigmoid(ln)).astype(_BF16)
            yield
    glu_ref[0:hist, :] = glu_ref[SEQ_TILE:SEQ_TILE + hist, :]
    for c0 in range(0, d, MXU_COLS):
        y = _dot(z_ref[...], w2_ref[:, c0:c0 + MXU_COLS]) + b2_ref[:, c0:c0 + MXU_COLS]
        dst_ref[:, c0:c0 + MXU_COLS] = h_ref[:, c0:c0 + MXU_COLS] + y
        yield


def _conformer_kernel(*refs):
    first_step = (pl.program_id(0) == 0) & (pl.program_id(1) == 0)
    _conformer_prepare(*refs, first_step=first_step, seq_tile=pl.program_id(1))
    for _ in _conformer_stage(*refs, seq_tile=pl.program_id(1)):
        pass


def _conformer(h, g, w1, b1, dw, bdw, lng, lnb, w2, b2, casts):
    d = h.shape[-1]
    hist = -(-(dw.shape[0] - 1) // SUBLANES) * SUBLANES
    return _tile_call(
        _conformer_kernel, "conformer", h, [g, w1, b1, dw, bdw, lng, lnb, w2, b2],
        [pltpu.VMEM((hist + SEQ_TILE, d), _F32),
         pltpu.VMEM((SUBLANES - 1, hist + CONV_BLOCK - SUBLANES, d), _F32),
         pltpu.VMEM((dw.shape[0], SUBLANES, d), _F32),
         pltpu.VMEM((SEQ_TILE, d), _BF16)],
        casts)


def _row(v):
    return v.reshape(1, -1)


def kernel(x, ln1_0, a0_w_in, a0_conv, a0_w_out, ln2_0, ffn0_w_gu, ffn0_w_down, ln1_1, b1_w_grp, b1_scale, ln2_1, ffn1_w_gu, ffn1_w_down, ln1_2, c2_w_pw1, c2_b_pw1, c2_dw, c2_b_dw, c2_ln_g, c2_ln_b, c2_w_pw2, c2_b_pw2, ln2_2, ffn2_w_gu, ffn2_w_down, ln1_3, a3_w_in, a3_conv, a3_w_out, ln2_3, ffn3_w_gu, ffn3_w_down, ln_f):
    h, (wgu0, wdown0) = _short_conv(
        x, _row(ln1_0), a0_w_in, a0_conv, a0_w_out, casts=(ffn0_w_gu, ffn0_w_down))
    h, (wgrp, wgu1, wdown1) = _ffn(
        h, _row(ln2_0), wgu0, wdown0, casts=(b1_w_grp, ffn1_w_gu, ffn1_w_down))
    h, (w_pw1, w_pw2) = _pool_ffn(
        h, _row(ln1_1), wgrp, _row(b1_scale), _row(ln2_1), wgu1, wdown1,
        casts=(c2_w_pw1, c2_w_pw2))
    h, (wgu2, wdown2) = _conformer(
        h, _row(ln1_2), w_pw1, _row(c2_b_pw1), c2_dw, _row(c2_b_dw), _row(c2_ln_g),
        _row(c2_ln_b), w_pw2, _row(c2_b_pw2), casts=(ffn2_w_gu, ffn2_w_down))
    h, (w_in3, w_out3) = _ffn(h, _row(ln2_2), wgu2, wdown2, casts=(a3_w_in, a3_w_out))
    h, (wgu3, wdown3) = _short_conv(
        h, _row(ln1_3), w_in3, a3_conv, w_out3, casts=(ffn3_w_gu, ffn3_w_down))
    return _ffn_final(h, _row(ln2_3), wgu3, wdown3, _row(ln_f))
```

```python
import math

import jax
import jax.numpy as jnp
from jax import lax
from jax.experimental import pallas as pl
from jax.experimental.pallas import tpu as pltpu

RMS_EPS = 1e-6
LN_EPS = 1e-5
POOL_WINDOWS = (2, 4, 8, 16)

SEQ_TILE = 512
FFN_TILE = 1024
CONV_TILE = 1024
FFN_CHUNK = 256
MXU_COLS = 256
SUBLANES = 8
BF16_ROWS = 16
VMEM_LIMIT = 58 * 1024 * 1024

_BF16 = jnp.bfloat16
_F32 = jnp.float32


def _dot(a, b):
    return jnp.dot(a, b, preferred_element_type=_F32)


def _rmsnorm(x, g):
    ms = jnp.mean(x * x, axis=-1, keepdims=True)
    return x * lax.rsqrt(ms + RMS_EPS) * g


def _resident(shape):
    zeros = (0,) * len(shape)
    return pl.BlockSpec(shape, lambda *_: zeros, pipeline_mode=pl.Buffered(1))


def _cast_blocks(shape, steps):
    if len(shape) > 2:
        return math.gcd(shape[0], steps)
    nb = steps
    while shape[0] % (nb * BF16_ROWS):
        nb //= 2
    return nb


def _tile_call(body, name, h, residents, scratch_shapes, casts=(), tile=SEQ_TILE):
    bn, s, d = h.shape
    tiles = s // tile
    steps = bn * tiles
    n_in = 1 + len(residents)
    n_cast = len(casts)
    row_tile = pl.BlockSpec((None, tile, d), lambda b, t: (b, t, 0))

    cast_specs = []
    for w in casts:
        nb = _cast_blocks(w.shape, steps)
        tail = (0,) * (w.ndim - 1)
        cast_specs.append(pl.BlockSpec(
            (w.shape[0] // nb,) + w.shape[1:],
            lambda b, t, nb=nb, tail=tail: (((b * tiles + t) * nb) // steps,) + tail))

    def wrapped(*refs):
        srcs = refs[n_in:n_in + n_cast]
        dsts = refs[n_in + n_cast + 1:n_in + 2 * n_cast + 1]
        for src, dst in zip(srcs, dsts):
            dst[...] = src[...].astype(_BF16)
        body(*refs[:n_in], refs[n_in + n_cast], *refs[n_in + 2 * n_cast + 1:])

    outs = pl.pallas_call(
        wrapped,
        grid=(bn, tiles),
        in_specs=[row_tile] + [_resident(r.shape) for r in residents] + cast_specs,
        out_specs=[row_tile] + cast_specs,
        out_shape=[jax.ShapeDtypeStruct(h.shape, h.dtype)]
        + [jax.ShapeDtypeStruct(w.shape, _BF16) for w in casts],
        scratch_shapes=scratch_shapes,
        compiler_params=pltpu.CompilerParams(
            dimension_semantics=("arbitrary", "arbitrary"),
            vmem_limit_bytes=VMEM_LIMIT),
        name=name,
    )(h, *residents, *casts)
    return outs[0], outs[1:]


def _alternate(a, n_a, b, n_b):
    i_a = i_b = 0
    while i_a < n_a or i_b < n_b:
        if i_b >= n_b or (i_a < n_a and i_a * n_b <= i_b * n_a):
            next(a)
            i_a += 1
        else:
            next(b)
            i_b += 1


def _layer_call(prepare, stage, pieces, head_pieces, name, h, residents, scratch_shapes, ffn,
                casts=()):
    bn, s, d = h.shape
    per_seq = s // SEQ_TILE
    tiles = bn * per_seq
    n_mix = 1 + len(residents)
    n_in = n_mix + len(ffn)
    n_cast = len(casts)

    def tile_spec(tile_of_step):
        def index(t):
            tile = tile_of_step(t)
            return tile // per_seq, tile % per_seq, 0
        return pl.BlockSpec((None, SEQ_TILE, d), index)

    mixer_tile = lambda t: jnp.minimum(t, tiles - 1)
    cast_specs = []
    for w in casts:
        nb = _cast_blocks(w.shape, tiles)
        tail = (0,) * (w.ndim - 1)
        cast_specs.append(pl.BlockSpec(
            (w.shape[0] // nb,) + w.shape[1:],
            lambda t, nb=nb, tail=tail: ((mixer_tile(t) * nb) // tiles,) + tail))

    def wrapped(*refs):
        srcs = refs[n_in:n_in + n_cast]
        o_ref = refs[n_in + n_cast]
        dsts = refs[n_in + n_cast + 1:n_in + 2 * n_cast + 1]
        mid_ref, act_ref = refs[n_in + 2 * n_cast + 1:n_in + 2 * n_cast + 3]
        t = pl.program_id(0)
        for src, dst in zip(srcs, dsts):
            dst[...] = src[...].astype(_BF16)

        @pl.when(t == 0)
        def _():
            mid_ref[...] = jnp.zeros(mid_ref.shape, _F32)

        stage_args = refs[:n_mix] + (mid_ref,) + refs[n_in + 2 * n_cast + 3:]
        seq_tile = mixer_tile(t) % per_seq
        prepare(*stage_args, first_step=t == 0, seq_tile=seq_tile)
        mixer = stage(*stage_args, seq_tile=seq_tile)
        for _ in range(head_pieces):
            next(mixer)
        swiglu = _swiglu_pieces(mid_ref, *refs[n_mix:n_in], act_ref, o_ref)
        _alternate(swiglu, _swiglu_piece_count(ffn[2].shape), mixer, pieces - head_pieces)

    w_down = ffn[2]
    outs = pl.pallas_call(
        wrapped,
        grid=(tiles + 1,),
        in_specs=[tile_spec(mixer_tile)] + [_resident(r.shape) for r in residents + list(ffn)]
        + cast_specs,
        out_specs=[tile_spec(lambda t: jnp.maximum(t - 1, 0))] + cast_specs,
        out_shape=[jax.ShapeDtypeStruct(h.shape, h.dtype)]
        + [jax.ShapeDtypeStruct(w.shape, _BF16) for w in casts],
        scratch_shapes=[pltpu.VMEM((SEQ_TILE, d), _F32), _act_scratch(w_down)] + scratch_shapes,
        compiler_params=pltpu.CompilerParams(
            dimension_semantics=("arbitrary",),
            vmem_limit_bytes=VMEM_LIMIT),
        name=name,
    )(h, *residents, *ffn, *casts)
    return outs[0], outs[1:]


def _swiglu_piece_count(w_down_shape):
    return 1 + w_down_shape[0] // FFN_CHUNK + w_down_shape[1] // MXU_COLS


def _swiglu_pieces(x_ref, g_ref, wgu_ref, wdown_ref, act_ref, o_ref, lnf_ref=None):
    f, d = wdown_ref.shape
    x = x_ref[...]
    o_ref[...] = x
    xn = _rmsnorm(x, g_ref[...]).astype(_BF16)
    yield
    for j in range(f // FFN_CHUNK):
        lo = j * FFN_CHUNK
        gate = _dot(xn, wgu_ref[:, lo:lo + FFN_CHUNK])
        up = _dot(xn, wgu_ref[:, f + lo:f + lo + FFN_CHUNK])
        act_ref[:, lo:lo + FFN_CHUNK] = (gate * jax.nn.sigmoid(gate) * up).astype(_BF16)
        yield
    for c0 in range(0, d, MXU_COLS):
        o_ref[:, c0:c0 + MXU_COLS] += _dot(act_ref[...], wdown_ref[:, c0:c0 + MXU_COLS])
        yield
    if lnf_ref is not None:
        o_ref[...] = _rmsnorm(o_ref[...], lnf_ref[...])
        yield


def _staggered(gens, lag):
    done = [False] * len(gens)
    turn = 0
    while not all(done):
        for i, gen in enumerate(gens):
            if turn >= i * lag and not done[i]:
                done[i] = next(gen, True) is True
        turn += 1


def _ffn_kernel(h_ref, g_ref, wgu_ref, wdown_ref, *rest):
    *lnf, o_ref, act_ref = rest
    half = h_ref.shape[0] // 2
    rows = [pl.ds(i * half, half) for i in range(2)]
    _staggered([_swiglu_pieces(h_ref.at[r], g_ref, wgu_ref, wdown_ref, act_ref.at[r],
                               o_ref.at[r], *lnf) for r in rows], lag=1)


def _act_scratch(w_down, tile=SEQ_TILE):
    return pltpu.VMEM((tile, w_down.shape[0]), _BF16)


def _ffn(h, g, w_gu, w_down, casts):
    return _tile_call(_ffn_kernel, "ffn", h, [g, w_gu, w_down],
                      [_act_scratch(w_down, FFN_TILE)], casts, tile=FFN_TILE)


def _ffn_final(h, g, w_gu, w_down, ln_f):
    return _tile_call(_ffn_kernel, "ffn_final", h, [g, w_gu, w_down, ln_f],
                      [_act_scratch(w_down, FFN_TILE)], tile=FFN_TILE)[0]


def _short_conv_pieces(rows, h_ref, g_ref, win_ref, cw_ref, wout_ref, o_ref, u_ref, y_ref):
    d = h_ref.shape[-1]
    k = cw_ref.shape[0]
    first = SUBLANES + rows.start
    xn = _rmsnorm(h_ref[rows, :], g_ref[...]).astype(_BF16)
    yield
    gates = []
    for c0 in range(0, d, MXU_COLS):
        cols = slice(c0, c0 + MXU_COLS)
        b, c, v = (_dot(xn, win_ref[:, i * d + c0:i * d + c0 + MXU_COLS]) for i in range(3))
        gates.append(b)
        u_ref[first:first + rows.size, cols] = c * v
        yield
    for gate_b, c0 in zip(gates, range(0, d, MXU_COLS)):
        cols = slice(c0, c0 + MXU_COLS)
        conv = cw_ref[k - 1:k, cols] * u_ref[first:first + rows.size, cols]
        for j in range(1, k):
            conv = conv + cw_ref[k - 1 - j:k - j, cols] * u_ref[first - j:first - j + rows.size, cols]
        y_ref[rows, cols] = (gate_b * conv).astype(_BF16)
        yield
    for c0 in range(0, d, MXU_COLS):
        cols = slice(c0, c0 + MXU_COLS)
        o_ref[rows, cols] = h_ref[rows, cols] + _dot(y_ref[rows, :], wout_ref[:, cols])
        yield


def _short_conv_kernel(h_ref, g_ref, win_ref, cw_ref, wout_ref, o_ref, u_ref, y_ref, *own_bf16):
    d = h_ref.shape[-1]
    tile = h_ref.shape[0]
    hist = SUBLANES

    if own_bf16:
        @pl.when((pl.program_id(0) == 0) & (pl.program_id(1) == 0))
        def _():
            for src, dst in zip((win_ref, wout_ref), own_bf16):
                dst[...] = src[...].astype(_BF16)
        win_ref, wout_ref = own_bf16

    @pl.when(pl.program_id(1) == 0)
    def _():
        u_ref[0:hist, :] = jnp.zeros((hist, d), _F32)

    halves = [pl.ds(i * (tile // 2), tile // 2) for i in range(2)]
    _staggered([_short_conv_pieces(rows, h_ref, g_ref, win_ref, cw_ref, wout_ref, o_ref,
                                   u_ref, y_ref) for rows in halves], lag=d // MXU_COLS)
    u_ref[0:hist, :] = u_ref[tile:tile + hist, :]


def _short_conv(h, g, w_in, conv_w, w_out, casts):
    d = h.shape[-1]
    own_bf16 = [pltpu.VMEM(w.shape, _BF16) for w in (w_in, w_out) if w.dtype != _BF16]
    return _tile_call(_short_conv_kernel, "short_conv", h, [g, w_in, conv_w, w_out],
                      [pltpu.VMEM((SUBLANES + CONV_TILE, d), _F32),
                       pltpu.VMEM((CONV_TILE, d), _BF16)] + own_bf16, casts, tile=CONV_TILE)


def _pool_prepare(h_ref, g_ref, wgrp_ref, scale_ref, dst_ref, xn_ref, *, first_step, seq_tile):
    @pl.when(seq_tile == 0)
    def _():
        xn_ref[0:max(POOL_WINDOWS), :] = jnp.zeros((max(POOL_WINDOWS), xn_ref.shape[1]), _F32)


def _pool_stage(h_ref, g_ref, wgrp_ref, scale_ref, dst_ref, xn_ref, *, seq_tile):
    d = h_ref.shape[-1]
    hist = max(POOL_WINDOWS)
    gw = d // len(POOL_WINDOWS)
    xn_ref[hist:hist + SEQ_TILE, :] = _rmsnorm(h_ref[...], g_ref[...])
    yield
    t = seq_tile * SEQ_TILE + lax.broadcasted_iota(jnp.int32, (SEQ_TILE, 1), 0)
    for gi, w in enumerate(POOL_WINDOWS):
        c0 = gi * gw
        cur = xn_ref[hist:hist + SEQ_TILE, c0:c0 + gw]
        tot = cur
        for j in range(1, w):
            tot = tot + xn_ref[hist - j:hist - j + SEQ_TILE, c0:c0 + gw]
        cnt = jnp.minimum(t + 1, w).astype(_F32)
        mixed = (tot / cnt - cur).astype(_BF16)
        y = _dot(mixed, wgrp_ref[gi]) * scale_ref[:, c0:c0 + gw]
        dst_ref[:, c0:c0 + gw] = h_ref[:, c0:c0 + gw] + y
        if gi == len(POOL_WINDOWS) - 1:
            xn_ref[0:hist, :] = xn_ref[SEQ_TILE:SEQ_TILE + hist, :]
        yield


def _pool_ffn(h, g1, w_grp, scale, g2, w_gu, w_down, casts):
    d = h.shape[-1]
    return _layer_call(_pool_prepare, _pool_stage, 1 + len(POOL_WINDOWS), 0, "pool_ffn", h,
                       [g1, w_grp, scale],
                       [pltpu.VMEM((max(POOL_WINDOWS) + SEQ_TILE, d), _F32)],
                       (g2, w_gu, w_down), casts)


CONF_ROWS = 16
HALO_ROWS = 256


LANES = 128
TAP_GROUP = 4


def _conformer_prepare(h_ref, g_ref, w1_ref, b1_ref, dw_ref, bdw_ref, lng_ref, lnb_ref,
                       w2_ref, b2_ref, dst_ref, ext_ref, carry_ref, dwb_ref, z_ref,
                       xs_ref, ys_ref, *, first_step, seq_tile):
    d = h_ref.shape[-1]

    @pl.when(first_step)
    def _():
        for j in range(dw_ref.shape[0]):
            dwb_ref[j] = jnp.broadcast_to(dw_ref[j:j + 1, :], (SUBLANES, d))

    @pl.when(seq_tile == 0)
    def _():
        carry_ref[...] = jnp.zeros(carry_ref.shape, _F32)


def _conformer_stage(h_ref, g_ref, w1_ref, b1_ref, dw_ref, bdw_ref, lng_ref, lnb_ref,
                     w2_ref, b2_ref, dst_ref, ext_ref, carry_ref, dwb_ref, z_ref,
                     xs_ref, ys_ref, *, seq_tile):
    del seq_tile
    d = h_ref.shape[-1]
    k = dw_ref.shape[0]
    seg = SEQ_TILE // SUBLANES
    slabs = range(d // LANES)

    xn = _rmsnorm(h_ref[...], g_ref[...])
    for c in slabs:
        for s in range(SUBLANES):
            xs_ref[c, pl.ds(s, seg, stride=SUBLANES), :] = (
                xn[seg * s:seg * (s + 1), LANES * c:LANES * (c + 1)])
    yield
    xn = jnp.concatenate([xs_ref[c] for c in slabs], axis=-1).astype(_BF16)
    for c0 in range(0, d, MXU_COLS):
        a = _dot(xn, w1_ref[:, c0:c0 + MXU_COLS]) + b1_ref[:, c0:c0 + MXU_COLS]
        gate = _dot(xn, w1_ref[:, d + c0:d + c0 + MXU_COLS]) + b1_ref[:, d + c0:d + c0 + MXU_COLS]
        ext_ref[HALO_ROWS:HALO_ROWS + SEQ_TILE, c0:c0 + MXU_COLS] = a * jax.nn.sigmoid(gate)
        yield

    last = ext_ref[SEQ_TILE:SEQ_TILE + HALO_ROWS, :]
    vregs = (HALO_ROWS // SUBLANES, SUBLANES, d)
    sublane = lax.broadcasted_iota(jnp.int32, (1, SUBLANES, d), 1)
    halo = jnp.where(sublane == 0,
                     pltpu.roll(carry_ref[...].reshape(vregs), 1, axis=1),
                     pltpu.roll(last.reshape(vregs), 1, axis=1))
    ext_ref[0:HALO_ROWS, :] = halo.reshape(HALO_ROWS, d)
    carry_ref[...] = last
    yield

    groups = CONF_ROWS // SUBLANES
    for r0 in range(0, SEQ_TILE, CONF_ROWS):
        acc = jnp.zeros((groups, SUBLANES, d), _F32) + bdw_ref[...]
        for m0 in range(0, k, TAP_GROUP):
            terms = []
            for m in range(m0, min(m0 + TAP_GROUP, k)):
                lo = HALO_ROWS + r0 - SUBLANES * m
                win = ext_ref[lo:lo + CONF_ROWS, :]
                terms.append(dwb_ref[k - 1 - m] * win.reshape(groups, SUBLANES, d))
            while len(terms) > 1:
                terms = [a + b for a, b in zip(terms[::2], terms[1::2])] + terms[len(terms) & ~1:]
            acc = acc + terms[0]
        acc = acc.reshape(CONF_ROWS, d)
        mu = jnp.mean(acc, axis=-1, keepdims=True)
        xc = acc - mu
        var = jnp.mean(xc * xc, axis=-1, keepdims=True)
        ln = xc * lax.rsqrt(var + LN_EPS) * lng_ref[...] + lnb_ref[...]
        z_ref[r0:r0 + CONF_ROWS, :] = (ln * jax.nn.sigmoid(ln)).astype(_BF16)
        yield
    for c0 in range(0, d, MXU_COLS):
        y = _dot(z_ref[...], w2_ref[:, c0:c0 + MXU_COLS]) + b2_ref[:, c0:c0 + MXU_COLS]
        for c in range(c0 // LANES, (c0 + MXU_COLS) // LANES):
            ys_ref[c] = y[:, LANES * c - c0:LANES * (c + 1) - c0]
            for s in range(SUBLANES):
                rows = slice(seg * s, seg * (s + 1))
                lanes = slice(LANES * c, LANES * (c + 1))
                dst_ref[rows, lanes] = (h_ref[rows, lanes]
                                        + ys_ref[c, pl.ds(s, seg, stride=SUBLANES), :])
        yield


def _conformer_kernel(*refs):
    first_step = (pl.program_id(0) == 0) & (pl.program_id(1) == 0)
    _conformer_prepare(*refs, first_step=first_step, seq_tile=pl.program_id(1))
    for _ in _conformer_stage(*refs, seq_tile=pl.program_id(1)):
        pass


def _conformer(h, g, w1, b1, dw, bdw, lng, lnb, w2, b2, casts):
    d = h.shape[-1]
    assert SUBLANES * (dw.shape[0] - 1) <= HALO_ROWS
    return _tile_call(
        _conformer_kernel, "conformer", h, [g, w1, b1, dw, bdw, lng, lnb, w2, b2],
        [pltpu.VMEM((HALO_ROWS + SEQ_TILE, d), _F32),
         pltpu.VMEM((HALO_ROWS, d), _F32),
         pltpu.VMEM((dw.shape[0], SUBLANES, d), _F32),
         pltpu.VMEM((SEQ_TILE, d), _BF16),
         pltpu.VMEM((d // LANES, SEQ_TILE, LANES), _F32),
         pltpu.VMEM((d // LANES, SEQ_TILE, LANES), _F32)],
        casts)


def _row(v):
    return v.reshape(1, -1)


def kernel(x, ln1_0, a0_w_in, a0_conv, a0_w_out, ln2_0, ffn0_w_gu, ffn0_w_down, ln1_1, b1_w_grp, b1_scale, ln2_1, ffn1_w_gu, ffn1_w_down, ln1_2, c2_w_pw1, c2_b_pw1, c2_dw, c2_b_dw, c2_ln_g, c2_ln_b, c2_w_pw2, c2_b_pw2, ln2_2, ffn2_w_gu, ffn2_w_down, ln1_3, a3_w_in, a3_conv, a3_w_out, ln2_3, ffn3_w_gu, ffn3_w_down, ln_f):
    h, (wgu0, wdown0) = _short_conv(
        x, _row(ln1_0), a0_w_in, a0_conv, a0_w_out, casts=(ffn0_w_gu, ffn0_w_down))
    h, (wgrp, wgu1, wdown1) = _ffn(
        h, _row(ln2_0), wgu0, wdown0, casts=(b1_w_grp, ffn1_w_gu, ffn1_w_down))
    h, (w_pw1, w_pw2) = _pool_ffn(
        h, _row(ln1_1), wgrp, _row(b1_scale), _row(ln2_1), wgu1, wdown1,
        casts=(c2_w_pw1, c2_w_pw2))
    h, (wgu2, wdown2) = _conformer(
        h, _row(ln1_2), w_pw1, _row(c2_b_pw1), c2_dw, _row(c2_b_dw), _row(c2_ln_g),
        _row(c2_ln_b), w_pw2, _row(c2_b_pw2), casts=(ffn2_w_gu, ffn2_w_down))
    h, (w_in3, w_out3) = _ffn(h, _row(ln2_2), wgu2, wdown2, casts=(a3_w_in, a3_w_out))
    h, (wgu3, wdown3) = _short_conv(
        h, _row(ln1_3), w_in3, a3_conv, w_out3, casts=(ffn3_w_gu, ffn3_w_down))
    return _ffn_final(h, _row(ln2_3), wgu3, wdown3, _row(ln_f))
```

```python
import math

import jax
import jax.numpy as jnp
from jax import lax
from jax.experimental import pallas as pl
from jax.experimental.pallas import tpu as pltpu

RMS_EPS = 1e-6
LN_EPS = 1e-5
POOL_WINDOWS = (2, 4, 8, 16)

SEQ_TILE = 512
FFN_TILE = 1024
CONV_TILE = 1024
FFN_CHUNK = 256
MXU_COLS = 256
SUBLANES = 8
BF16_ROWS = 16
VMEM_LIMIT = 58 * 1024 * 1024

_BF16 = jnp.bfloat16
_F32 = jnp.float32


def _dot(a, b):
    return jnp.dot(a, b, preferred_element_type=_F32)


def _rmsnorm(x, g):
    ms = jnp.mean(x * x, axis=-1, keepdims=True)
    return x * lax.rsqrt(ms + RMS_EPS) * g


def _resident(shape):
    zeros = (0,) * len(shape)
    return pl.BlockSpec(shape, lambda *_: zeros, pipeline_mode=pl.Buffered(1))


def _cast_blocks(shape, steps):
    if len(shape) > 2:
        return math.gcd(shape[0], steps)
    nb = steps
    while shape[0] % (nb * BF16_ROWS):
        nb //= 2
    return nb


def _tile_call(body, name, h, residents, scratch_shapes, casts=(), tile=SEQ_TILE):
    bn, s, d = h.shape
    tiles = s // tile
    steps = bn * tiles
    n_in = 1 + len(residents)
    n_cast = len(casts)
    row_tile = pl.BlockSpec((None, tile, d), lambda b, t: (b, t, 0))

    cast_specs = []
    for w in casts:
        nb = _cast_blocks(w.shape, steps)
        tail = (0,) * (w.ndim - 1)
        cast_specs.append(pl.BlockSpec(
            (w.shape[0] // nb,) + w.shape[1:],
            lambda b, t, nb=nb, tail=tail: (((b * tiles + t) * nb) // steps,) + tail))

    def wrapped(*refs):
        srcs = refs[n_in:n_in + n_cast]
        dsts = refs[n_in + n_cast + 1:n_in + 2 * n_cast + 1]
        for src, dst in zip(srcs, dsts):
            dst[...] = src[...].astype(_BF16)
        body(*refs[:n_in], refs[n_in + n_cast], *refs[n_in + 2 * n_cast + 1:])

    outs = pl.pallas_call(
        wrapped,
        grid=(bn, tiles),
        in_specs=[row_tile] + [_resident(r.shape) for r in residents] + cast_specs,
        out_specs=[row_tile] + cast_specs,
        out_shape=[jax.ShapeDtypeStruct(h.shape, h.dtype)]
        + [jax.ShapeDtypeStruct(w.shape, _BF16) for w in casts],
        scratch_shapes=scratch_shapes,
        compiler_params=pltpu.CompilerParams(
            dimension_semantics=("arbitrary", "arbitrary"),
            vmem_limit_bytes=VMEM_LIMIT),
        name=name,
    )(h, *residents, *casts)
    return outs[0], outs[1:]


def _staggered(gens, lag):
    done = [False] * len(gens)
    turn = 0
    while not all(done):
        for i, gen in enumerate(gens):
            if turn >= i * lag and not done[i]:
                done[i] = next(gen, True) is True
        turn += 1


def _halves(tile):
    return [pl.ds(i * (tile // 2), tile // 2) for i in range(2)]


def _swiglu_pieces(x_ref, g_ref, wgu_ref, wdown_ref, act_ref, o_ref, lnf_ref=None):
    f, d = wdown_ref.shape
    x = x_ref[...]
    o_ref[...] = x
    xn = _rmsnorm(x, g_ref[...]).astype(_BF16)
    yield
    for j in range(f // FFN_CHUNK):
        lo = j * FFN_CHUNK
        gate = _dot(xn, wgu_ref[:, lo:lo + FFN_CHUNK])
        up = _dot(xn, wgu_ref[:, f + lo:f + lo + FFN_CHUNK])
        act_ref[:, lo:lo + FFN_CHUNK] = (gate * jax.nn.sigmoid(gate) * up).astype(_BF16)
        yield
    for c0 in range(0, d, MXU_COLS):
        o_ref[:, c0:c0 + MXU_COLS] += _dot(act_ref[...], wdown_ref[:, c0:c0 + MXU_COLS])
        yield
    if lnf_ref is not None:
        o_ref[...] = _rmsnorm(o_ref[...], lnf_ref[...])
        yield


def _ffn_kernel(h_ref, g_ref, wgu_ref, wdown_ref, *rest):
    *lnf, o_ref, act_ref = rest
    _staggered([_swiglu_pieces(h_ref.at[r], g_ref, wgu_ref, wdown_ref, act_ref.at[r],
                               o_ref.at[r], *lnf) for r in _halves(h_ref.shape[0])], lag=1)


def _act_scratch(w_down, tile):
    return pltpu.VMEM((tile, w_down.shape[0]), _BF16)


def _ffn(h, g, w_gu, w_down, casts):
    return _tile_call(_ffn_kernel, "ffn", h, [g, w_gu, w_down],
                      [_act_scratch(w_down, FFN_TILE)], casts, tile=FFN_TILE)


def _ffn_final(h, g, w_gu, w_down, ln_f):
    return _tile_call(_ffn_kernel, "ffn_final", h, [g, w_gu, w_down, ln_f],
                      [_act_scratch(w_down, FFN_TILE)], tile=FFN_TILE)[0]


def _short_conv_pieces(rows, h_ref, g_ref, win_ref, cw_ref, wout_ref, o_ref, u_ref, y_ref):
    d = h_ref.shape[-1]
    k = cw_ref.shape[0]
    first = SUBLANES + rows.start
    xn = _rmsnorm(h_ref[rows, :], g_ref[...]).astype(_BF16)
    yield
    gates = []
    for c0 in range(0, d, MXU_COLS):
        cols = slice(c0, c0 + MXU_COLS)
        b, c, v = (_dot(xn, win_ref[:, i * d + c0:i * d + c0 + MXU_COLS]) for i in range(3))
        gates.append(b)
        u_ref[first:first + rows.size, cols] = c * v
        yield
    for gate_b, c0 in zip(gates, range(0, d, MXU_COLS)):
        cols = slice(c0, c0 + MXU_COLS)
        conv = cw_ref[k - 1:k, cols] * u_ref[first:first + rows.size, cols]
        for j in range(1, k):
            conv = conv + cw_ref[k - 1 - j:k - j, cols] * u_ref[first - j:first - j + rows.size, cols]
        y_ref[rows, cols] = (gate_b * conv).astype(_BF16)
        yield
    for c0 in range(0, d, MXU_COLS):
        cols = slice(c0, c0 + MXU_COLS)
        o_ref[rows, cols] = h_ref[rows, cols] + _dot(y_ref[rows, :], wout_ref[:, cols])
        yield


def _short_conv_kernel(h_ref, g_ref, win_ref, cw_ref, wout_ref, o_ref, u_ref, y_ref, *own_bf16):
    d = h_ref.shape[-1]
    tile = h_ref.shape[0]
    hist = SUBLANES

    if own_bf16:
        @pl.when((pl.program_id(0) == 0) & (pl.program_id(1) == 0))
        def _():
            for src, dst in zip((win_ref, wout_ref), own_bf16):
                dst[...] = src[...].astype(_BF16)
        win_ref, wout_ref = own_bf16

    @pl.when(pl.program_id(1) == 0)
    def _():
        u_ref[0:hist, :] = jnp.zeros((hist, d), _F32)

    _staggered([_short_conv_pieces(rows, h_ref, g_ref, win_ref, cw_ref, wout_ref, o_ref,
                                   u_ref, y_ref) for rows in _halves(tile)], lag=d // MXU_COLS)
    u_ref[0:hist, :] = u_ref[tile:tile + hist, :]


def _short_conv(h, g, w_in, conv_w, w_out, casts):
    d = h.shape[-1]
    own_bf16 = [pltpu.VMEM(w.shape, _BF16) for w in (w_in, w_out) if w.dtype != _BF16]
    return _tile_call(_short_conv_kernel, "short_conv", h, [g, w_in, conv_w, w_out],
                      [pltpu.VMEM((SUBLANES + CONV_TILE, d), _F32),
                       pltpu.VMEM((CONV_TILE, d), _BF16)] + own_bf16, casts, tile=CONV_TILE)


def _pool_stage(t, h_ref, g_ref, wgrp_ref, scale_ref, dst_ref, xn_ref):
    d = h_ref.shape[-1]
    hist = max(POOL_WINDOWS)
    gw = d // len(POOL_WINDOWS)
    xn_ref[hist:hist + SEQ_TILE, :] = _rmsnorm(h_ref[...], g_ref[...])
    yield
    pos = t * SEQ_TILE + lax.broadcasted_iota(jnp.int32, (SEQ_TILE, 1), 0)
    for gi, w in enumerate(POOL_WINDOWS):
        cols = slice(gi * gw, (gi + 1) * gw)
        cur = xn_ref[hist:hist + SEQ_TILE, cols]
        tot = cur
        for j in range(1, w):
            tot = tot + xn_ref[hist - j:hist - j + SEQ_TILE, cols]
        cnt = jnp.minimum(pos + 1, w).astype(_F32)
        mixed = (tot / cnt - cur).astype(_BF16)
        y = _dot(mixed, wgrp_ref[gi]) * scale_ref[:, cols]
        dst_ref[:, cols] = h_ref[:, cols] + y
        if gi == len(POOL_WINDOWS) - 1:
            xn_ref[0:hist, :] = xn_ref[SEQ_TILE:SEQ_TILE + hist, :]
        yield


def _alternate(a, n_a, b, n_b):
    i_a = i_b = 0
    while i_a < n_a or i_b < n_b:
        if i_b >= n_b or (i_a < n_a and i_a * n_b <= i_b * n_a):
            next(a)
            i_a += 1
        else:
            next(b)
            i_b += 1


def _pool_ffn(h, g1, w_grp, scale, g2, w_gu, w_down, casts):
    bn, s, d = h.shape
    f = w_down.shape[0]
    per_seq = s // SEQ_TILE
    tiles = bn * per_seq
    hist = max(POOL_WINDOWS)
    residents = [g1, w_grp, scale, g2, w_gu, w_down]
    n_in = 1 + len(residents)
    n_cast = len(casts)
    mixer_tile = lambda t: jnp.minimum(t, tiles - 1)

    def tile_spec(tile_of_step):
        def index(t):
            tile = tile_of_step(t)
            return tile // per_seq, tile % per_seq, 0
        return pl.BlockSpec((None, SEQ_TILE, d), index)

    cast_specs = []
    for w in casts:
        nb = _cast_blocks(w.shape, tiles)
        tail = (0,) * (w.ndim - 1)
        cast_specs.append(pl.BlockSpec(
            (w.shape[0] // nb,) + w.shape[1:],
            lambda t, nb=nb, tail=tail: ((mixer_tile(t) * nb) // tiles,) + tail))

    def body(h_ref, g1_ref, wgrp_ref, scale_ref, g2_ref, wgu_ref, wdown_ref, *rest):
        srcs, o_ref, dsts = rest[:n_cast], rest[n_cast], rest[n_cast + 1:2 * n_cast + 1]
        mid_ref, act_ref, xn_ref = rest[2 * n_cast + 1:]
        t = pl.program_id(0)
        for src, dst in zip(srcs, dsts):
            dst[...] = src[...].astype(_BF16)

        @pl.when(t == 0)
        def _():
            mid_ref[...] = jnp.zeros(mid_ref.shape, _F32)

        seq_tile = mixer_tile(t) % per_seq

        @pl.when(seq_tile == 0)
        def _():
            xn_ref[0:hist, :] = jnp.zeros((hist, d), _F32)

        swiglu = _swiglu_pieces(mid_ref, g2_ref, wgu_ref, wdown_ref, act_ref, o_ref)
        mixer = _pool_stage(seq_tile, h_ref, g1_ref, wgrp_ref, scale_ref, mid_ref, xn_ref)
        _alternate(swiglu, 1 + f // FFN_CHUNK + d // MXU_COLS, mixer, 1 + len(POOL_WINDOWS))

    outs = pl.pallas_call(
        body,
        grid=(tiles + 1,),
        in_specs=[tile_spec(mixer_tile)] + [_resident(r.shape) for r in residents] + cast_specs,
        out_specs=[tile_spec(lambda t: jnp.maximum(t - 1, 0))] + cast_specs,
        out_shape=[jax.ShapeDtypeStruct(h.shape, h.dtype)]
        + [jax.ShapeDtypeStruct(w.shape, _BF16) for w in casts],
        scratch_shapes=[pltpu.VMEM((SEQ_TILE, d), _F32), _act_scratch(w_down, SEQ_TILE),
                        pltpu.VMEM((hist + SEQ_TILE, d), _F32)],
        compiler_params=pltpu.CompilerParams(
            dimension_semantics=("arbitrary",),
            vmem_limit_bytes=VMEM_LIMIT),
        name="pool_ffn",
    )(h, *residents, *casts)
    return outs[0], outs[1:]


CONF_ROWS = 16
HALO_ROWS = 256
LANES = 128
TAP_GROUP = 4


def _conformer_kernel(h_ref, g_ref, w1_ref, b1_ref, dw_ref, bdw_ref, lng_ref, lnb_ref,
                      w2_ref, b2_ref, o_ref, ext_ref, carry_ref, dwb_ref, z_ref, xs_ref, ys_ref):
    d = h_ref.shape[-1]
    k = dw_ref.shape[0]
    seg = SEQ_TILE // SUBLANES
    slabs = range(d // LANES)

    @pl.when((pl.program_id(0) == 0) & (pl.program_id(1) == 0))
    def _():
        for j in range(k):
            dwb_ref[j] = jnp.broadcast_to(dw_ref[j:j + 1, :], (SUBLANES, d))

    @pl.when(pl.program_id(1) == 0)
    def _():
        carry_ref[...] = jnp.zeros(carry_ref.shape, _F32)

    xn = _rmsnorm(h_ref[...], g_ref[...])
    for c in slabs:
        for s in range(SUBLANES):
            xs_ref[c, pl.ds(s, seg, stride=SUBLANES), :] = (
                xn[seg * s:seg * (s + 1), LANES * c:LANES * (c + 1)])
    xn = jnp.concatenate([xs_ref[c] for c in slabs], axis=-1).astype(_BF16)
    for c0 in range(0, d, MXU_COLS):
        a = _dot(xn, w1_ref[:, c0:c0 + MXU_COLS]) + b1_ref[:, c0:c0 + MXU_COLS]
        gate = _dot(xn, w1_ref[:, d + c0:d + c0 + MXU_COLS]) + b1_ref[:, d + c0:d + c0 + MXU_COLS]
        ext_ref[HALO_ROWS:HALO_ROWS + SEQ_TILE, c0:c0 + MXU_COLS] = a * jax.nn.sigmoid(gate)

    last = ext_ref[SEQ_TILE:SEQ_TILE + HALO_ROWS, :]
    vregs = (HALO_ROWS // SUBLANES, SUBLANES, d)
    sublane = lax.broadcasted_iota(jnp.int32, (1, SUBLANES, d), 1)
    halo = jnp.where(sublane == 0,
                     pltpu.roll(carry_ref[...].reshape(vregs), 1, axis=1),
                     pltpu.roll(last.reshape(vregs), 1, axis=1))
    ext_ref[0:HALO_ROWS, :] = halo.reshape(HALO_ROWS, d)
    carry_ref[...] = last

    groups = CONF_ROWS // SUBLANES
    for r0 in range(0, SEQ_TILE, CONF_ROWS):
        acc = jnp.zeros((groups, SUBLANES, d), _F32) + bdw_ref[...]
        for m0 in range(0, k, TAP_GROUP):
            terms = []
            for m in range(m0, min(m0 + TAP_GROUP, k)):
                lo = HALO_ROWS + r0 - SUBLANES * m
                win = ext_ref[lo:lo + CONF_ROWS, :]
                terms.append(dwb_ref[k - 1 - m] * win.reshape(groups, SUBLANES, d))
            while len(terms) > 1:
                terms = [a + b for a, b in zip(terms[::2], terms[1::2])] + terms[len(terms) & ~1:]
            acc = acc + terms[0]
        acc = acc.reshape(CONF_ROWS, d)
        mu = jnp.mean(acc, axis=-1, keepdims=True)
        xc = acc - mu
        var = jnp.mean(xc * xc, axis=-1, keepdims=True)
        ln = xc * lax.rsqrt(var + LN_EPS) * lng_ref[...] + lnb_ref[...]
        z_ref[r0:r0 + CONF_ROWS, :] = (ln * jax.nn.sigmoid(ln)).astype(_BF16)
    for c0 in range(0, d, MXU_COLS):
        y = _dot(z_ref[...], w2_ref[:, c0:c0 + MXU_COLS]) + b2_ref[:, c0:c0 + MXU_COLS]
        for c in range(c0 // LANES, (c0 + MXU_COLS) // LANES):
            ys_ref[c] = y[:, LANES * c - c0:LANES * (c + 1) - c0]
            for s in range(SUBLANES):
                rows = slice(seg * s, seg * (s + 1))
                lanes = slice(LANES * c, LANES * (c + 1))
                o_ref[rows, lanes] = (h_ref[rows, lanes]
                                      + ys_ref[c, pl.ds(s, seg, stride=SUBLANES), :])


def _conformer(h, g, w1, b1, dw, bdw, lng, lnb, w2, b2, casts):
    d = h.shape[-1]
    assert SUBLANES * (dw.shape[0] - 1) <= HALO_ROWS
    return _tile_call(
        _conformer_kernel, "conformer", h, [g, w1, b1, dw, bdw, lng, lnb, w2, b2],
        [pltpu.VMEM((HALO_ROWS + SEQ_TILE, d), _F32),
         pltpu.VMEM((HALO_ROWS, d), _F32),
         pltpu.VMEM((dw.shape[0], SUBLANES, d), _F32),
         pltpu.VMEM((SEQ_TILE, d), _BF16),
         pltpu.VMEM((d // LANES, SEQ_TILE, LANES), _F32),
         pltpu.VMEM((d // LANES, SEQ_TILE, LANES), _F32)],
        casts)


def _row(v):
    return v.reshape(1, -1)


def kernel(x, ln1_0, a0_w_in, a0_conv, a0_w_out, ln2_0, ffn0_w_gu, ffn0_w_down, ln1_1, b1_w_grp, b1_scale, ln2_1, ffn1_w_gu, ffn1_w_down, ln1_2, c2_w_pw1, c2_b_pw1, c2_dw, c2_b_dw, c2_ln_g, c2_ln_b, c2_w_pw2, c2_b_pw2, ln2_2, ffn2_w_gu, ffn2_w_down, ln1_3, a3_w_in, a3_conv, a3_w_out, ln2_3, ffn3_w_gu, ffn3_w_down, ln_f):
    h, (wgu0, wdown0) = _short_conv(
        x, _row(ln1_0), a0_w_in, a0_conv, a0_w_out, casts=(ffn0_w_gu, ffn0_w_down))
    h, (wgrp, wgu1, wdown1) = _ffn(
        h, _row(ln2_0), wgu0, wdown0, casts=(b1_w_grp, ffn1_w_gu, ffn1_w_down))
    h, (w_pw1, w_pw2) = _pool_ffn(
        h, _row(ln1_1), wgrp, _row(b1_scale), _row(ln2_1), wgu1, wdown1,
        casts=(c2_w_pw1, c2_w_pw2))
    h, (wgu2, wdown2) = _conformer(
        h, _row(ln1_2), w_pw1, _row(c2_b_pw1), c2_dw, _row(c2_b_dw), _row(c2_ln_g),
        _row(c2_ln_b), w_pw2, _row(c2_b_pw2), casts=(ffn2_w_gu, ffn2_w_down))
    h, (w_in3, w_out3) = _ffn(h, _row(ln2_2), wgu2, wdown2, casts=(a3_w_in, a3_w_out))
    h, (wgu3, wdown3) = _short_conv(
        h, _row(ln1_3), w_in3, a3_conv, w_out3, casts=(ffn3_w_gu, ffn3_w_down))
    return _ffn_final(h, _row(ln2_3), wgu3, wdown3, _row(ln_f))
```

```python
import math

import jax
import jax.numpy as jnp
from jax import lax
from jax.experimental import pallas as pl
from jax.experimental.pallas import tpu as pltpu

RMS_EPS = 1e-6
LN_EPS = 1e-5
POOL_WINDOWS = (2, 4, 8, 16)

SEQ_TILE = 512
FFN_TILE = 1024
CONV_TILE = 1024
FFN_CHUNK = 256
MXU_COLS = 256
SUBLANES = 8
BF16_ROWS = 16
VMEM_LIMIT = 58 * 1024 * 1024

_BF16 = jnp.bfloat16
_F32 = jnp.float32


def _dot(a, b):
    return jnp.dot(a, b, preferred_element_type=_F32)


def _rmsnorm(x, g):
    ms = jnp.mean(x * x, axis=-1, keepdims=True)
    return x * lax.rsqrt(ms + RMS_EPS) * g


def _resident(shape):
    zeros = (0,) * len(shape)
    return pl.BlockSpec(shape, lambda *_: zeros, pipeline_mode=pl.Buffered(1))


def _cast_blocks(shape, steps):
    if len(shape) > 2:
        return math.gcd(shape[0], steps)
    nb = steps
    while shape[0] % (nb * BF16_ROWS):
        nb //= 2
    return nb


def _tile_call(body, name, h, residents, scratch_shapes, casts=(), tile=SEQ_TILE):
    bn, s, d = h.shape
    tiles = s // tile
    steps = bn * tiles
    n_in = 1 + len(residents)
    n_cast = len(casts)
    row_tile = pl.BlockSpec((None, tile, d), lambda b, t: (b, t, 0))

    cast_specs = []
    for w in casts:
        nb = _cast_blocks(w.shape, steps)
        tail = (0,) * (w.ndim - 1)
        cast_specs.append(pl.BlockSpec(
            (w.shape[0] // nb,) + w.shape[1:],
            lambda b, t, nb=nb, tail=tail: (((b * tiles + t) * nb) // steps,) + tail))

    def wrapped(*refs):
        srcs = refs[n_in:n_in + n_cast]
        dsts = refs[n_in + n_cast + 1:n_in + 2 * n_cast + 1]
        for src, dst in zip(srcs, dsts):
            dst[...] = src[...].astype(_BF16)
        body(*refs[:n_in], refs[n_in + n_cast], *refs[n_in + 2 * n_cast + 1:])

    outs = pl.pallas_call(
        wrapped,
        grid=(bn, tiles),
        in_specs=[row_tile] + [_resident(r.shape) for r in residents] + cast_specs,
        out_specs=[row_tile] + cast_specs,
        out_shape=[jax.ShapeDtypeStruct(h.shape, h.dtype)]
        + [jax.ShapeDtypeStruct(w.shape, _BF16) for w in casts],
        scratch_shapes=scratch_shapes,
        compiler_params=pltpu.CompilerParams(
            dimension_semantics=("arbitrary", "arbitrary"),
            vmem_limit_bytes=VMEM_LIMIT),
        name=name,
    )(h, *residents, *casts)
    return outs[0], outs[1:]


def _staggered(gens, lag):
    done = [False] * len(gens)
    turn = 0
    while not all(done):
        for i, gen in enumerate(gens):
            if turn >= i * lag and not done[i]:
                done[i] = next(gen, True) is True
        turn += 1


def _halves(tile):
    return [pl.ds(i * (tile // 2), tile // 2) for i in range(2)]


def _swiglu_pieces(x_ref, g_ref, wgu_ref, wdown_ref, act_ref, o_ref, lnf_ref=None):
    f, d = wdown_ref.shape
    x = x_ref[...]
    o_ref[...] = x
    xn = _rmsnorm(x, g_ref[...]).astype(_BF16)
    yield
    for j in range(f // FFN_CHUNK):
        lo = j * FFN_CHUNK
        gate = _dot(xn, wgu_ref[:, lo:lo + FFN_CHUNK])
        up = _dot(xn, wgu_ref[:, f + lo:f + lo + FFN_CHUNK])
        act_ref[:, lo:lo + FFN_CHUNK] = (gate * jax.nn.sigmoid(gate) * up).astype(_BF16)
        yield
    for c0 in range(0, d, MXU_COLS):
        o_ref[:, c0:c0 + MXU_COLS] += _dot(act_ref[...], wdown_ref[:, c0:c0 + MXU_COLS])
        yield
    if lnf_ref is not None:
        o_ref[...] = _rmsnorm(o_ref[...], lnf_ref[...])
        yield


def _ffn_kernel(h_ref, g_ref, wgu_ref, wdown_ref, *rest):
    *lnf, o_ref, act_ref = rest
    _staggered([_swiglu_pieces(h_ref.at[r], g_ref, wgu_ref, wdown_ref, act_ref.at[r],
                               o_ref.at[r], *lnf) for r in _halves(h_ref.shape[0])], lag=1)


def _act_scratch(w_down, tile):
    return pltpu.VMEM((tile, w_down.shape[0]), _BF16)


def _ffn(h, g, w_gu, w_down, casts):
    return _tile_call(_ffn_kernel, "ffn", h, [g, w_gu, w_down],
                      [_act_scratch(w_down, FFN_TILE)], casts, tile=FFN_TILE)


def _ffn_final(h, g, w_gu, w_down, ln_f):
    return _tile_call(_ffn_kernel, "ffn_final", h, [g, w_gu, w_down, ln_f],
                      [_act_scratch(w_down, FFN_TILE)], tile=FFN_TILE)[0]


def _short_conv_pieces(rows, h_ref, g_ref, win_ref, cw_ref, wout_ref, o_ref, u_ref, y_ref):
    d = h_ref.shape[-1]
    k = cw_ref.shape[0]
    first = SUBLANES + rows.start
    xn = _rmsnorm(h_ref[rows, :], g_ref[...]).astype(_BF16)
    yield
    gates = []
    for c0 in range(0, d, MXU_COLS):
        cols = slice(c0, c0 + MXU_COLS)
        b, c, v = (_dot(xn, win_ref[:, i * d + c0:i * d + c0 + MXU_COLS]) for i in range(3))
        gates.append(b)
        u_ref[first:first + rows.size, cols] = c * v
        yield
    for gate_b, c0 in zip(gates, range(0, d, MXU_COLS)):
        cols = slice(c0, c0 + MXU_COLS)
        conv = cw_ref[k - 1:k, cols] * u_ref[first:first + rows.size, cols]
        for j in range(1, k):
            conv = conv + cw_ref[k - 1 - j:k - j, cols] * u_ref[first - j:first - j + rows.size, cols]
        y_ref[rows, cols] = (gate_b * conv).astype(_BF16)
        yield
    for c0 in range(0, d, MXU_COLS):
        cols = slice(c0, c0 + MXU_COLS)
        o_ref[rows, cols] = h_ref[rows, cols] + _dot(y_ref[rows, :], wout_ref[:, cols])
        yield


def _short_conv_kernel(h_ref, g_ref, win_ref, cw_ref, wout_ref, o_ref, u_ref, y_ref, *own_bf16):
    d = h_ref.shape[-1]
    tile = h_ref.shape[0]
    hist = SUBLANES

    if own_bf16:
        @pl.when((pl.program_id(0) == 0) & (pl.program_id(1) == 0))
        def _():
            for src, dst in zip((win_ref, wout_ref), own_bf16):
                dst[...] = src[...].astype(_BF16)
        win_ref, wout_ref = own_bf16

    @pl.when(pl.program_id(1) == 0)
    def _():
        u_ref[0:hist, :] = jnp.zeros((hist, d), _F32)

    _staggered([_short_conv_pieces(rows, h_ref, g_ref, win_ref, cw_ref, wout_ref, o_ref,
                                   u_ref, y_ref) for rows in _halves(tile)], lag=d // MXU_COLS)
    u_ref[0:hist, :] = u_ref[tile:tile + hist, :]


def _short_conv(h, g, w_in, conv_w, w_out, casts):
    d = h.shape[-1]
    own_bf16 = [pltpu.VMEM(w.shape, _BF16) for w in (w_in, w_out) if w.dtype != _BF16]
    return _tile_call(_short_conv_kernel, "short_conv", h, [g, w_in, conv_w, w_out],
                      [pltpu.VMEM((SUBLANES + CONV_TILE, d), _F32),
                       pltpu.VMEM((CONV_TILE, d), _BF16)] + own_bf16, casts, tile=CONV_TILE)


def _pool_stage(t, h_ref, g_ref, wgrp_ref, scale_ref, dst_ref, xn_ref):
    d = h_ref.shape[-1]
    hist = max(POOL_WINDOWS)
    gw = d // len(POOL_WINDOWS)
    xn_ref[hist:hist + SEQ_TILE, :] = _rmsnorm(h_ref[...], g_ref[...])
    yield
    pos = t * SEQ_TILE + lax.broadcasted_iota(jnp.int32, (SEQ_TILE, 1), 0)
    for gi, w in enumerate(POOL_WINDOWS):
        cols = slice(gi * gw, (gi + 1) * gw)
        cur = xn_ref[hist:hist + SEQ_TILE, cols]
        tot = cur
        for j in range(1, w):
            tot = tot + xn_ref[hist - j:hist - j + SEQ_TILE, cols]
        cnt = jnp.minimum(pos + 1, w).astype(_F32)
        mixed = (tot / cnt - cur).astype(_BF16)
        y = _dot(mixed, wgrp_ref[gi]) * scale_ref[:, cols]
        dst_ref[:, cols] = h_ref[:, cols] + y
        if gi == len(POOL_WINDOWS) - 1:
            xn_ref[0:hist, :] = xn_ref[SEQ_TILE:SEQ_TILE + hist, :]
        yield


def _alternate(a, n_a, b, n_b):
    i_a = i_b = 0
    while i_a < n_a or i_b < n_b:
        if i_b >= n_b or (i_a < n_a and i_a * n_b <= i_b * n_a):
            next(a)
            i_a += 1
        else:
            next(b)
            i_b += 1


def _pool_ffn(h, g1, w_grp, scale, g2, w_gu, w_down, casts):
    bn, s, d = h.shape
    f = w_down.shape[0]
    per_seq = s // SEQ_TILE
    tiles = bn * per_seq
    hist = max(POOL_WINDOWS)
    residents = [g1, w_grp, scale, g2, w_gu, w_down]
    n_in = 1 + len(residents)
    n_cast = len(casts)
    mixer_tile = lambda t: jnp.minimum(t, tiles - 1)

    def tile_spec(tile_of_step):
        def index(t):
            tile = tile_of_step(t)
            return tile // per_seq, tile % per_seq, 0
        return pl.BlockSpec((None, SEQ_TILE, d), index)

    cast_specs = []
    for w in casts:
        nb = _cast_blocks(w.shape, tiles)
        tail = (0,) * (w.ndim - 1)
        cast_specs.append(pl.BlockSpec(
            (w.shape[0] // nb,) + w.shape[1:],
            lambda t, nb=nb, tail=tail: ((mixer_tile(t) * nb) // tiles,) + tail))

    def body(h_ref, g1_ref, wgrp_ref, scale_ref, g2_ref, wgu_ref, wdown_ref, *rest):
        srcs, o_ref, dsts = rest[:n_cast], rest[n_cast], rest[n_cast + 1:2 * n_cast + 1]
        mid_ref, act_ref, xn_ref = rest[2 * n_cast + 1:]
        t = pl.program_id(0)
        for src, dst in zip(srcs, dsts):
            dst[...] = src[...].astype(_BF16)

        @pl.when(t == 0)
        def _():
            mid_ref[...] = jnp.zeros(mid_ref.shape, _F32)

        seq_tile = mixer_tile(t) % per_seq

        @pl.when(seq_tile == 0)
        def _():
            xn_ref[0:hist, :] = jnp.zeros((hist, d), _F32)

        swiglu = _swiglu_pieces(mid_ref, g2_ref, wgu_ref, wdown_ref, act_ref, o_ref)
        mixer = _pool_stage(seq_tile, h_ref, g1_ref, wgrp_ref, scale_ref, mid_ref, xn_ref)
        _alternate(swiglu, 1 + f // FFN_CHUNK + d // MXU_COLS, mixer, 1 + len(POOL_WINDOWS))

    outs = pl.pallas_call(
        body,
        grid=(tiles + 1,),
        in_specs=[tile_spec(mixer_tile)] + [_resident(r.shape) for r in residents] + cast_specs,
        out_specs=[tile_spec(lambda t: jnp.maximum(t - 1, 0))] + cast_specs,
        out_shape=[jax.ShapeDtypeStruct(h.shape, h.dtype)]
        + [jax.ShapeDtypeStruct(w.shape, _BF16) for w in casts],
        scratch_shapes=[pltpu.VMEM((SEQ_TILE, d), _F32), _act_scratch(w_down, SEQ_TILE),
                        pltpu.VMEM((hist + SEQ_TILE, d), _F32)],
        compiler_params=pltpu.CompilerParams(
            dimension_semantics=("arbitrary",),
            vmem_limit_bytes=VMEM_LIMIT),
        name="pool_ffn",
    )(h, *residents, *casts)
    return outs[0], outs[1:]


CONF_ROWS = 16
HALO_ROWS = 256
LANES = 128
TAP_GROUP = 2


def _conformer_kernel(h_ref, g_ref, w1_ref, b1_ref, dw_ref, bdw_ref, lng_ref, lnb_ref,
                      w2_ref, b2_ref, o_ref, ext_ref, carry_ref, dwb_ref, z_ref, xs_ref, ys_ref):
    d = h_ref.shape[-1]
    k = dw_ref.shape[0]
    seg = SEQ_TILE // SUBLANES
    slabs = range(d // LANES)

    @pl.when((pl.program_id(0) == 0) & (pl.program_id(1) == 0))
    def _():
        for j in range(k):
            dwb_ref[j] = jnp.broadcast_to(dw_ref[j:j + 1, :], (SUBLANES, d))

    @pl.when(pl.program_id(1) == 0)
    def _():
        carry_ref[...] = jnp.zeros(carry_ref.shape, _F32)

    xn = _rmsnorm(h_ref[...], g_ref[...])
    for c in slabs:
        for s in range(SUBLANES):
            xs_ref[c, pl.ds(s, seg, stride=SUBLANES), :] = (
                xn[seg * s:seg * (s + 1), LANES * c:LANES * (c + 1)])
    xn = jnp.concatenate([xs_ref[c] for c in slabs], axis=-1).astype(_BF16)
    for c0 in range(0, d, MXU_COLS):
        a = _dot(xn, w1_ref[:, c0:c0 + MXU_COLS]) + b1_ref[:, c0:c0 + MXU_COLS]
        gate = _dot(xn, w1_ref[:, d + c0:d + c0 + MXU_COLS]) + b1_ref[:, d + c0:d + c0 + MXU_COLS]
        ext_ref[HALO_ROWS:HALO_ROWS + SEQ_TILE, c0:c0 + MXU_COLS] = a * jax.nn.sigmoid(gate)

    last = ext_ref[SEQ_TILE:SEQ_TILE + HALO_ROWS, :]
    vregs = (HALO_ROWS // SUBLANES, SUBLANES, d)
    sublane = lax.broadcasted_iota(jnp.int32, (1, SUBLANES, d), 1)
    halo = jnp.where(sublane == 0,
                     pltpu.roll(carry_ref[...].reshape(vregs), 1, axis=1),
                     pltpu.roll(last.reshape(vregs), 1, axis=1))
    ext_ref[0:HALO_ROWS, :] = halo.reshape(HALO_ROWS, d)
    carry_ref[...] = last

    groups = CONF_ROWS // SUBLANES
    for r0 in range(0, SEQ_TILE, CONF_ROWS):
        acc = jnp.zeros((groups, SUBLANES, d), _F32) + bdw_ref[...]
        for m0 in range(0, k, TAP_GROUP):
            terms = []
            for m in range(m0, min(m0 + TAP_GROUP, k)):
                lo = HALO_ROWS + r0 - SUBLANES * m
                win = ext_ref[lo:lo + CONF_ROWS, :]
                terms.append(dwb_ref[k - 1 - m] * win.reshape(groups, SUBLANES, d))
            while len(terms) > 1:
                terms = [a + b for a, b in zip(terms[::2], terms[1::2])] + terms[len(terms) & ~1:]
            acc = acc + terms[0]
        acc = acc.reshape(CONF_ROWS, d)
        mu = jnp.mean(acc, axis=-1, keepdims=True)
        xc = acc - mu
        var = jnp.mean(xc * xc, axis=-1, keepdims=True)
        ln = xc * lax.rsqrt(var + LN_EPS) * lng_ref[...] + lnb_ref[...]
        z_ref[r0:r0 + CONF_ROWS, :] = (ln * jax.nn.sigmoid(ln)).astype(_BF16)
    for c0 in range(0, d, MXU_COLS):
        y = _dot(z_ref[...], w2_ref[:, c0:c0 + MXU_COLS]) + b2_ref[:, c0:c0 + MXU_COLS]
        for c in range(c0 // LANES, (c0 + MXU_COLS) // LANES):
            ys_ref[c] = y[:, LANES * c - c0:LANES * (c + 1) - c0]
            for s in range(SUBLANES):
                rows = slice(seg * s, seg * (s + 1))
                lanes = slice(LANES * c, LANES * (c + 1))
                o_ref[rows, lanes] = (h_ref[rows, lanes]
                                      + ys_ref[c, pl.ds(s, seg, stride=SUBLANES), :])


def _conformer(h, g, w1, b1, dw, bdw, lng, lnb, w2, b2, casts):
    d = h.shape[-1]
    assert SUBLANES * (dw.shape[0] - 1) <= HALO_ROWS
    return _tile_call(
        _conformer_kernel, "conformer", h, [g, w1, b1, dw, bdw, lng, lnb, w2, b2],
        [pltpu.VMEM((HALO_ROWS + SEQ_TILE, d), _F32),
         pltpu.VMEM((HALO_ROWS, d), _F32),
         pltpu.VMEM((dw.shape[0], SUBLANES, d), _F32),
         pltpu.VMEM((SEQ_TILE, d), _BF16),
         pltpu.VMEM((d // LANES, SEQ_TILE, LANES), _F32),
         pltpu.VMEM((d // LANES, SEQ_TILE, LANES), _F32)],
        casts)


def _row(v):
    return v.reshape(1, -1)


def kernel(x, ln1_0, a0_w_in, a0_conv, a0_w_out, ln2_0, ffn0_w_gu, ffn0_w_down, ln1_1, b1_w_grp, b1_scale, ln2_1, ffn1_w_gu, ffn1_w_down, ln1_2, c2_w_pw1, c2_b_pw1, c2_dw, c2_b_dw, c2_ln_g, c2_ln_b, c2_w_pw2, c2_b_pw2, ln2_2, ffn2_w_gu, ffn2_w_down, ln1_3, a3_w_in, a3_conv, a3_w_out, ln2_3, ffn3_w_gu, ffn3_w_down, ln_f):
    h, (wgu0, wdown0) = _short_conv(
        x, _row(ln1_0), a0_w_in, a0_conv, a0_w_out, casts=(ffn0_w_gu, ffn0_w_down))
    h, (wgrp, wgu1, wdown1) = _ffn(
        h, _row(ln2_0), wgu0, wdown0, casts=(b1_w_grp, ffn1_w_gu, ffn1_w_down))
    h, (w_pw1, w_pw2) = _pool_ffn(
        h, _row(ln1_1), wgrp, _row(b1_scale), _row(ln2_1), wgu1, wdown1,
        casts=(c2_w_pw1, c2_w_pw2))
    h, (wgu2, wdown2) = _conformer(
        h, _row(ln1_2), w_pw1, _row(c2_b_pw1), c2_dw, _row(c2_b_dw), _row(c2_ln_g),
        _row(c2_ln_b), w_pw2, _row(c2_b_pw2), casts=(ffn2_w_gu, ffn2_w_down))
    h, (w_in3, w_out3) = _ffn(h, _row(ln2_2), wgu2, wdown2, casts=(a3_w_in, a3_w_out))
    h, (wgu3, wdown3) = _short_conv(
        h, _row(ln1_3), w_in3, a3_conv, w_out3, casts=(ffn3_w_gu, ffn3_w_down))
    return _ffn_final(h, _row(ln2_3), wgu3, wdown3, _row(ln_f))
```

```python
import math

import jax
import jax.numpy as jnp
from jax import lax
from jax.experimental import pallas as pl
from jax.experimental.pallas import tpu as pltpu

RMS_EPS = 1e-6
LN_EPS = 1e-5
POOL_WINDOWS = (2, 4, 8, 16)

SEQ_TILE = 512
FFN_TILE = 1024
CONV_TILE = 1024
FFN_CHUNK = 256
MXU_COLS = 256
SUBLANES = 8
BF16_ROWS = 16
VMEM_LIMIT = 58 * 1024 * 1024

_BF16 = jnp.bfloat16
_F32 = jnp.float32


def _dot(a, b):
    return jnp.dot(a, b, preferred_element_type=_F32)


def _rmsnorm(x, g):
    ms = jnp.mean(x * x, axis=-1, keepdims=True)
    return x * lax.rsqrt(ms + RMS_EPS) * g


def _resident(shape):
    zeros = (0,) * len(shape)
    return pl.BlockSpec(shape, lambda *_: zeros, pipeline_mode=pl.Buffered(1))


def _cast_blocks(shape, steps):
    if len(shape) > 2:
        return math.gcd(shape[0], steps)
    nb = steps
    while shape[0] % (nb * BF16_ROWS):
        nb //= 2
    return nb


def _tile_call(body, name, h, residents, scratch_shapes, casts=(), tile=SEQ_TILE):
    bn, s, d = h.shape
    tiles = s // tile
    steps = bn * tiles
    n_in = 1 + len(residents)
    n_cast = len(casts)
    row_tile = pl.BlockSpec((None, tile, d), lambda b, t: (b, t, 0))

    cast_specs = []
    for w in casts:
        nb = _cast_blocks(w.shape, steps)
        tail = (0,) * (w.ndim - 1)
        cast_specs.append(pl.BlockSpec(
            (w.shape[0] // nb,) + w.shape[1:],
            lambda b, t, nb=nb, tail=tail: (((b * tiles + t) * nb) // steps,) + tail))

    def wrapped(*refs):
        srcs = refs[n_in:n_in + n_cast]
        dsts = refs[n_in + n_cast + 1:n_in + 2 * n_cast + 1]
        for src, dst in zip(srcs, dsts):
            dst[...] = src[...].astype(_BF16)
        body(*refs[:n_in], refs[n_in + n_cast], *refs[n_in + 2 * n_cast + 1:])

    outs = pl.pallas_call(
        wrapped,
        grid=(bn, tiles),
        in_specs=[row_tile] + [_resident(r.shape) for r in residents] + cast_specs,
        out_specs=[row_tile] + cast_specs,
        out_shape=[jax.ShapeDtypeStruct(h.shape, h.dtype)]
        + [jax.ShapeDtypeStruct(w.shape, _BF16) for w in casts],
        scratch_shapes=scratch_shapes,
        compiler_params=pltpu.CompilerParams(
            dimension_semantics=("arbitrary", "arbitrary"),
            vmem_limit_bytes=VMEM_LIMIT),
        name=name,
    )(h, *residents, *casts)
    return outs[0], outs[1:]


def _staggered(gens, lag):
    done = [False] * len(gens)
    turn = 0
    while not all(done):
        for i, gen in enumerate(gens):
            if turn >= i * lag and not done[i]:
                done[i] = next(gen, True) is True
        turn += 1


def _halves(tile):
    return [pl.ds(i * (tile // 2), tile // 2) for i in range(2)]


def _swiglu_pieces(x_ref, g_ref, wgu_ref, wdown_ref, act_ref, o_ref, lnf_ref=None):
    f, d = wdown_ref.shape
    x = x_ref[...]
    o_ref[...] = x
    xn = _rmsnorm(x, g_ref[...]).astype(_BF16)
    yield
    for j in range(f // FFN_CHUNK):
        lo = j * FFN_CHUNK
        gate = _dot(xn, wgu_ref[:, lo:lo + FFN_CHUNK])
        up = _dot(xn, wgu_ref[:, f + lo:f + lo + FFN_CHUNK])
        act_ref[:, lo:lo + FFN_CHUNK] = (gate * jax.nn.sigmoid(gate) * up).astype(_BF16)
        yield
    for c0 in range(0, d, MXU_COLS):
        o_ref[:, c0:c0 + MXU_COLS] += _dot(act_ref[...], wdown_ref[:, c0:c0 + MXU_COLS])
        yield
    if lnf_ref is not None:
        o_ref[...] = _rmsnorm(o_ref[...], lnf_ref[...])
        yield


def _ffn_kernel(h_ref, g_ref, wgu_ref, wdown_ref, *rest):
    *lnf, o_ref, act_ref = rest
    _staggered([_swiglu_pieces(h_ref.at[r], g_ref, wgu_ref, wdown_ref, act_ref.at[r],
                               o_ref.at[r], *lnf) for r in _halves(h_ref.shape[0])], lag=1)


def _act_scratch(w_down, tile):
    return pltpu.VMEM((tile, w_down.shape[0]), _BF16)


def _ffn(h, g, w_gu, w_down, casts):
    return _tile_call(_ffn_kernel, "ffn", h, [g, w_gu, w_down],
                      [_act_scratch(w_down, FFN_TILE)], casts, tile=FFN_TILE)


def _ffn_final(h, g, w_gu, w_down, ln_f):
    return _tile_call(_ffn_kernel, "ffn_final", h, [g, w_gu, w_down, ln_f],
                      [_act_scratch(w_down, FFN_TILE)], tile=FFN_TILE)[0]


def _short_conv_pieces(rows, h_ref, g_ref, win_ref, cw_ref, wout_ref, o_ref, u_ref, y_ref):
    d = h_ref.shape[-1]
    k = cw_ref.shape[0]
    first = SUBLANES + rows.start
    xn = _rmsnorm(h_ref[rows, :], g_ref[...]).astype(_BF16)
    yield
    gates = []
    for c0 in range(0, d, MXU_COLS):
        cols = slice(c0, c0 + MXU_COLS)
        b, c, v = (_dot(xn, win_ref[:, i * d + c0:i * d + c0 + MXU_COLS]) for i in range(3))
        gates.append(b)
        u_ref[first:first + rows.size, cols] = c * v
        yield
    for gate_b, c0 in zip(gates, range(0, d, MXU_COLS)):
        cols = slice(c0, c0 + MXU_COLS)
        conv = cw_ref[k - 1:k, cols] * u_ref[first:first + rows.size, cols]
        for j in range(1, k):
            conv = conv + cw_ref[k - 1 - j:k - j, cols] * u_ref[first - j:first - j + rows.size, cols]
        y_ref[rows, cols] = (gate_b * conv).astype(_BF16)
        yield
    for c0 in range(0, d, MXU_COLS):
        cols = slice(c0, c0 + MXU_COLS)
        o_ref[rows, cols] = h_ref[rows, cols] + _dot(y_ref[rows, :], wout_ref[:, cols])
        yield


def _short_conv_kernel(h_ref, g_ref, win_ref, cw_ref, wout_ref, o_ref, u_ref, y_ref, *own_bf16):
    d = h_ref.shape[-1]
    tile = h_ref.shape[0]
    hist = SUBLANES

    if own_bf16:
        @pl.when((pl.program_id(0) == 0) & (pl.program_id(1) == 0))
        def _():
            for src, dst in zip((win_ref, wout_ref), own_bf16):
                dst[...] = src[...].astype(_BF16)
        win_ref, wout_ref = own_bf16

    @pl.when(pl.program_id(1) == 0)
    def _():
        u_ref[0:hist, :] = jnp.zeros((hist, d), _F32)

    _staggered([_short_conv_pieces(rows, h_ref, g_ref, win_ref, cw_ref, wout_ref, o_ref,
                                   u_ref, y_ref) for rows in _halves(tile)], lag=d // MXU_COLS)
    u_ref[0:hist, :] = u_ref[tile:tile + hist, :]


def _short_conv(h, g, w_in, conv_w, w_out, casts):
    d = h.shape[-1]
    own_bf16 = [pltpu.VMEM(w.shape, _BF16) for w in (w_in, w_out) if w.dtype != _BF16]
    return _tile_call(_short_conv_kernel, "short_conv", h, [g, w_in, conv_w, w_out],
                      [pltpu.VMEM((SUBLANES + CONV_TILE, d), _F32),
                       pltpu.VMEM((CONV_TILE, d), _BF16)] + own_bf16, casts, tile=CONV_TILE)


def _pool_stage(t, h_ref, g_ref, wgrp_ref, scale_ref, dst_ref, xn_ref):
    d = h_ref.shape[-1]
    hist = max(POOL_WINDOWS)
    gw = d // len(POOL_WINDOWS)
    xn_ref[hist:hist + SEQ_TILE, :] = _rmsnorm(h_ref[...], g_ref[...])
    yield
    pos = t * SEQ_TILE + lax.broadcasted_iota(jnp.int32, (SEQ_TILE, 1), 0)
    for gi, w in enumerate(POOL_WINDOWS):
        cols = slice(gi * gw, (gi + 1) * gw)
        cur = xn_ref[hist:hist + SEQ_TILE, cols]
        tot = cur
        for j in range(1, w):
            tot = tot + xn_ref[hist - j:hist - j + SEQ_TILE, cols]
        cnt = jnp.minimum(pos + 1, w).astype(_F32)
        mixed = (tot / cnt - cur).astype(_BF16)
        y = _dot(mixed, wgrp_ref[gi]) * scale_ref[:, cols]
        dst_ref[:, cols] = h_ref[:, cols] + y
        if gi == len(POOL_WINDOWS) - 1:
            xn_ref[0:hist, :] = xn_ref[SEQ_TILE:SEQ_TILE + hist, :]
        yield


def _alternate(a, n_a, b, n_b):
    i_a = i_b = 0
    while i_a < n_a or i_b < n_b:
        if i_b >= n_b or (i_a < n_a and i_a * n_b <= i_b * n_a):
            next(a)
            i_a += 1
        else:
            next(b)
            i_b += 1


def _pool_ffn(h, g1, w_grp, scale, g2, w_gu, w_down, casts):
    bn, s, d = h.shape
    f = w_down.shape[0]
    per_seq = s // SEQ_TILE
    tiles = bn * per_seq
    hist = max(POOL_WINDOWS)
    residents = [g1, w_grp, scale, g2, w_gu, w_down]
    n_in = 1 + len(residents)
    n_cast = len(casts)
    mixer_tile = lambda t: jnp.minimum(t, tiles - 1)

    def tile_spec(tile_of_step):
        def index(t):
            tile = tile_of_step(t)
            return tile // per_seq, tile % per_seq, 0
        return pl.BlockSpec((None, SEQ_TILE, d), index)

    cast_specs = []
    for w in casts:
        nb = _cast_blocks(w.shape, tiles)
        tail = (0,) * (w.ndim - 1)
        cast_specs.append(pl.BlockSpec(
            (w.shape[0] // nb,) + w.shape[1:],
            lambda t, nb=nb, tail=tail: ((mixer_tile(t) * nb) // tiles,) + tail))

    def body(h_ref, g1_ref, wgrp_ref, scale_ref, g2_ref, wgu_ref, wdown_ref, *rest):
        srcs, o_ref, dsts = rest[:n_cast], rest[n_cast], rest[n_cast + 1:2 * n_cast + 1]
        mid_ref, act_ref, xn_ref = rest[2 * n_cast + 1:]
        t = pl.program_id(0)
        for src, dst in zip(srcs, dsts):
            dst[...] = src[...].astype(_BF16)

        @pl.when(t == 0)
        def _():
            mid_ref[...] = jnp.zeros(mid_ref.shape, _F32)

        seq_tile = mixer_tile(t) % per_seq

        @pl.when(seq_tile == 0)
        def _():
            xn_ref[0:hist, :] = jnp.zeros((hist, d), _F32)

        swiglu = _swiglu_pieces(mid_ref, g2_ref, wgu_ref, wdown_ref, act_ref, o_ref)
        mixer = _pool_stage(seq_tile, h_ref, g1_ref, wgrp_ref, scale_ref, mid_ref, xn_ref)
        _alternate(swiglu, 1 + f // FFN_CHUNK + d // MXU_COLS, mixer, 1 + len(POOL_WINDOWS))

    outs = pl.pallas_call(
        body,
        grid=(tiles + 1,),
        in_specs=[tile_spec(mixer_tile)] + [_resident(r.shape) for r in residents] + cast_specs,
        out_specs=[tile_spec(lambda t: jnp.maximum(t - 1, 0))] + cast_specs,
        out_shape=[jax.ShapeDtypeStruct(h.shape, h.dtype)]
        + [jax.ShapeDtypeStruct(w.shape, _BF16) for w in casts],
        scratch_shapes=[pltpu.VMEM((SEQ_TILE, d), _F32), _act_scratch(w_down, SEQ_TILE),
                        pltpu.VMEM((hist + SEQ_TILE, d), _F32)],
        compiler_params=pltpu.CompilerParams(
            dimension_semantics=("arbitrary",),
            vmem_limit_bytes=VMEM_LIMIT),
        name="pool_ffn",
    )(h, *residents, *casts)
    return outs[0], outs[1:]


CONF_ROWS = 16
HALO_ROWS = 256
LANES = 128
TAP_GROUP = 2


def _conformer_kernel(h_ref, g_ref, w1_ref, b1_ref, dw_ref, bdw_ref, lng_ref, lnb_ref,
                      w2_ref, b2_ref, o_ref, ext_ref, carry_ref, dwb_ref, z_ref, xs_ref, ys_ref):
    d = h_ref.shape[-1]
    k = dw_ref.shape[0]
    seg = SEQ_TILE // SUBLANES
    slabs = range(d // LANES)

    @pl.when((pl.program_id(0) == 0) & (pl.program_id(1) == 0))
    def _():
        for j in range(k):
            dwb_ref[j] = jnp.broadcast_to(dw_ref[j:j + 1, :], (SUBLANES, d))

    @pl.when(pl.program_id(1) == 0)
    def _():
        carry_ref[...] = jnp.zeros(carry_ref.shape, _F32)

    xn = _rmsnorm(h_ref[...], g_ref[...])
    for c in slabs:
        for s in range(SUBLANES):
            xs_ref[c, pl.ds(s, seg, stride=SUBLANES), :] = (
                xn[seg * s:seg * (s + 1), LANES * c:LANES * (c + 1)])
    xn = jnp.concatenate([xs_ref[c] for c in slabs], axis=-1).astype(_BF16)
    for c0 in range(0, d, MXU_COLS):
        a = _dot(xn, w1_ref[:, c0:c0 + MXU_COLS]) + b1_ref[:, c0:c0 + MXU_COLS]
        gate = _dot(xn, w1_ref[:, d + c0:d + c0 + MXU_COLS]) + b1_ref[:, d + c0:d + c0 + MXU_COLS]
        ext_ref[HALO_ROWS:HALO_ROWS + SEQ_TILE, c0:c0 + MXU_COLS] = a * jax.nn.sigmoid(gate)

    last = ext_ref[SEQ_TILE:SEQ_TILE + HALO_ROWS, :]
    vregs = (HALO_ROWS // SUBLANES, SUBLANES, d)
    sublane = lax.broadcasted_iota(jnp.int32, (1, SUBLANES, d), 1)
    halo = jnp.where(sublane == 0,
                     pltpu.roll(carry_ref[...].reshape(vregs), 1, axis=1),
                     pltpu.roll(last.reshape(vregs), 1, axis=1))
    ext_ref[0:HALO_ROWS, :] = halo.reshape(HALO_ROWS, d)
    carry_ref[...] = last

    groups = CONF_ROWS // SUBLANES
    half = SEQ_TILE // 2

    def pointwise2(rows, c0):
        y = _dot(z_ref[rows, :], w2_ref[:, c0:c0 + MXU_COLS]) + b2_ref[:, c0:c0 + MXU_COLS]
        for c in range(c0 // LANES, (c0 + MXU_COLS) // LANES):
            ys_ref[c, rows, :] = y[:, LANES * c - c0:LANES * (c + 1) - c0]

    early = {half + (j + 1) * (half // (d // MXU_COLS)) - CONF_ROWS: j * MXU_COLS
             for j in range(d // MXU_COLS)}
    for r0 in range(0, SEQ_TILE, CONF_ROWS):
        acc = jnp.zeros((groups, SUBLANES, d), _F32) + bdw_ref[...]
        for m0 in range(0, k, TAP_GROUP):
            terms = []
            for m in range(m0, min(m0 + TAP_GROUP, k)):
                lo = HALO_ROWS + r0 - SUBLANES * m
                win = ext_ref[lo:lo + CONF_ROWS, :]
                terms.append(dwb_ref[k - 1 - m] * win.reshape(groups, SUBLANES, d))
            while len(terms) > 1:
                terms = [a + b for a, b in zip(terms[::2], terms[1::2])] + terms[len(terms) & ~1:]
            acc = acc + terms[0]
        acc = acc.reshape(CONF_ROWS, d)
        mu = jnp.mean(acc, axis=-1, keepdims=True)
        xc = acc - mu
        var = jnp.mean(xc * xc, axis=-1, keepdims=True)
        ln = xc * lax.rsqrt(var + LN_EPS) * lng_ref[...] + lnb_ref[...]
        z_ref[r0:r0 + CONF_ROWS, :] = (ln * jax.nn.sigmoid(ln)).astype(_BF16)
        if r0 in early:
            pointwise2(slice(0, half), early[r0])
    for c0 in range(0, d, MXU_COLS):
        pointwise2(slice(half, SEQ_TILE), c0)
        for c in range(c0 // LANES, (c0 + MXU_COLS) // LANES):
            for s in range(SUBLANES):
                rows = slice(seg * s, seg * (s + 1))
                lanes = slice(LANES * c, LANES * (c + 1))
                o_ref[rows, lanes] = (h_ref[rows, lanes]
                                      + ys_ref[c, pl.ds(s, seg, stride=SUBLANES), :])


def _conformer(h, g, w1, b1, dw, bdw, lng, lnb, w2, b2, casts):
    d = h.shape[-1]
    assert SUBLANES * (dw.shape[0] - 1) <= HALO_ROWS
    return _tile_call(
        _conformer_kernel, "conformer", h, [g, w1, b1, dw, bdw, lng, lnb, w2, b2],
        [pltpu.VMEM((HALO_ROWS + SEQ_TILE, d), _F32),
         pltpu.VMEM((HALO_ROWS, d), _F32),
         pltpu.VMEM((dw.shape[0], SUBLANES, d), _F32),
         pltpu.VMEM((SEQ_TILE, d), _BF16),
         pltpu.VMEM((d // LANES, SEQ_TILE, LANES), _F32),
         pltpu.VMEM((d // LANES, SEQ_TILE, LANES), _F32)],
        casts)


def _row(v):
    return v.reshape(1, -1)


def kernel(x, ln1_0, a0_w_in, a0_conv, a0_w_out, ln2_0, ffn0_w_gu, ffn0_w_down, ln1_1, b1_w_grp, b1_scale, ln2_1, ffn1_w_gu, ffn1_w_down, ln1_2, c2_w_pw1, c2_b_pw1, c2_dw, c2_b_dw, c2_ln_g, c2_ln_b, c2_w_pw2, c2_b_pw2, ln2_2, ffn2_w_gu, ffn2_w_down, ln1_3, a3_w_in, a3_conv, a3_w_out, ln2_3, ffn3_w_gu, ffn3_w_down, ln_f):
    h, (wgu0, wdown0) = _short_conv(
        x, _row(ln1_0), a0_w_in, a0_conv, a0_w_out, casts=(ffn0_w_gu, ffn0_w_down))
    h, (wgrp, wgu1, wdown1) = _ffn(
        h, _row(ln2_0), wgu0, wdown0, casts=(b1_w_grp, ffn1_w_gu, ffn1_w_down))
    h, (w_pw1, w_pw2) = _pool_ffn(
        h, _row(ln1_1), wgrp, _row(b1_scale), _row(ln2_1), wgu1, wdown1,
        casts=(c2_w_pw1, c2_w_pw2))
    h, (wgu2, wdown2) = _conformer(
        h, _row(ln1_2), w_pw1, _row(c2_b_pw1), c2_dw, _row(c2_b_dw), _row(c2_ln_g),
        _row(c2_ln_b), w_pw2, _row(c2_b_pw2), casts=(ffn2_w_gu, ffn2_w_down))
    h, (w_in3, w_out3) = _ffn(h, _row(ln2_2), wgu2, wdown2, casts=(a3_w_in, a3_w_out))
    h, (wgu3, wdown3) = _short_conv(
        h, _row(ln1_3), w_in3, a3_conv, w_out3, casts=(ffn3_w_gu, ffn3_w_down))
    return _ffn_final(h, _row(ln2_3), wgu3, wdown3, _row(ln_f))
```
